```python
import jax, jax.numpy as jnp
from jax import lax
import numpy as np

D_MODEL = 1024
BATCH = 16
SEQ = 4096
DEPTH = 1
DEC_BATCH = 32
DEC_SEQ = 16
PAST_LEN = 4096

CHUNK = 64
Q_BLOCK = 128
N_HEADS = D_MODEL // 128
QK_NOPE = 64
QK_ROPE = 32
V_DIM = 64
KV_LORA = D_MODEL // 4
Q_LORA = 3 * KV_LORA
CONV_CH = D_MODEL // 2
CONV_W = 31
D_FF = 4 * D_MODEL
PLE_DIM = 256
ROPE_THETA = 10000.0
RMS_EPS = 1e-6
LN_EPS = 1e-5
SM_SCALE = (QK_NOPE + QK_ROPE) ** -0.5
NEG_INF = -1e30

OFF_KV = Q_LORA
OFF_KR = OFF_KV + KV_LORA
OFF_CONV = OFF_KR + QK_ROPE
OFF_GATE = OFF_CONV + 2 * CONV_CH
IN_W = OFF_GATE + 2 * D_MODEL

kernel_name = 'mla_conformer_parallel_stream_encoder_step'


def rmsnorm(x, g):
    xf = x.astype(jnp.float32)
    y = xf * lax.rsqrt(jnp.mean(xf * xf, axis=-1, keepdims=True) + RMS_EPS)
    return (y * g.astype(jnp.float32)).astype(x.dtype)


def layernorm(x, g, b):
    xf = x.astype(jnp.float32)
    mu = jnp.mean(xf, axis=-1, keepdims=True)
    xc = xf - mu
    y = xc * lax.rsqrt(jnp.mean(xc * xc, axis=-1, keepdims=True) + LN_EPS)
    return (y * g.astype(jnp.float32) + b.astype(jnp.float32)).astype(x.dtype)


def rope(x, pos):
    half = x.shape[-1] // 2
    inv = ROPE_THETA ** (-jnp.arange(half, dtype=jnp.float32) / half)
    ang = pos.astype(jnp.float32)[:, None] * inv[None, :]
    cos = jnp.cos(ang)[:, None, :]
    sin = jnp.sin(ang)[:, None, :]
    x1 = x[..., :half].astype(jnp.float32)
    x2 = x[..., half:].astype(jnp.float32)
    out = jnp.concatenate([x1 * cos - x2 * sin, x2 * cos + x1 * sin], axis=-1)
    return out.astype(x.dtype)


def attend(q_lat, q_rope, lat, kr, q_pos, k_pos):
    s = (jnp.einsum('bqhc,bkc->bhqk', q_lat, lat)
         + jnp.einsum('bqhr,bkr->bhqk', q_rope, kr)).astype(jnp.float32) * SM_SCALE
    mask = (k_pos // CHUNK)[None, :] <= (q_pos // CHUNK)[:, None]
    s = jnp.where(mask, s, NEG_INF)
    p = jax.nn.softmax(s, axis=-1).astype(lat.dtype)
    return jnp.einsum('bhqk,bkc->bqhc', p, lat)


def blocked_attend(q_lat, q_rope, lat, kr, q_pos, k_pos):
    B, T, H, C = q_lat.shape
    if T <= Q_BLOCK or T % Q_BLOCK != 0:
        return attend(q_lat, q_rope, lat, kr, q_pos, k_pos)
    nb = T // Q_BLOCK
    ql = q_lat.reshape(B, nb, Q_BLOCK, H, C).transpose(1, 0, 2, 3, 4)
    qr = q_rope.reshape(B, nb, Q_BLOCK, H, QK_ROPE).transpose(1, 0, 2, 3, 4)
    qp = q_pos.reshape(nb, Q_BLOCK)
    out = lax.map(lambda a: attend(a[0], a[1], lat, kr, a[2], k_pos), (ql, qr, qp))
    return out.transpose(1, 0, 2, 3, 4).reshape(B, T, H, C)


def encoder_layer(x, p_emb, q_pos, cache_lat, cache_kr, cache_cv,
                  norm_mix_g, w_in, q_norm_g, w_uq, kv_norm_g, w_uk, w_uv, w_attn_out,
                  conv_w, conv_b, conv_ln_g, conv_ln_b, w_conv_out, w_out,
                  norm_ffn_g, w_ff_up, w_ff_down, ple_norm_g, w_ple_gate, w_ple_proj):
    B, T, _ = x.shape
    h = rmsnorm(x, norm_mix_g)
    z = h @ w_in
    c_q = z[..., :OFF_KV]
    c_kv = z[..., OFF_KV:OFF_KR]
    k_r = z[..., OFF_KR:OFF_CONV]
    u2 = z[..., OFF_CONV:OFF_GATE]
    g = z[..., OFF_GATE:]

    q = jnp.einsum('bsc,chd->bshd', rmsnorm(c_q, q_norm_g), w_uq)
    q_nope = q[..., :QK_NOPE]
    q_rope = rope(q[..., QK_NOPE:], q_pos)
    q_lat = jnp.einsum('bshd,chd->bshc', q_nope, w_uk)
    latent = rmsnorm(c_kv, kv_norm_g)
    k_rope = rope(k_r[:, :, None, :], q_pos)[:, :, 0, :]
    if cache_lat is None:
        lat_all, kr_all, k_pos = latent, k_rope, q_pos
        u_hist = jnp.zeros((B, CONV_W - 1, CONV_CH), x.dtype)
    else:
        lat_all = jnp.concatenate([cache_lat, latent], axis=1)
        kr_all = jnp.concatenate([cache_kr, k_rope], axis=1)
        k_pos = jnp.arange(cache_lat.shape[1] + T)
        u_hist = cache_cv
    o_lat = blocked_attend(q_lat, q_rope, lat_all, kr_all, q_pos, k_pos)
    o = jnp.einsum('bshc,chd->bshd', o_lat, w_uv).reshape(B, T, N_HEADS * V_DIM)
    attn_branch = o @ w_attn_out

    u = u2[..., :CONV_CH] * jax.nn.sigmoid(u2[..., CONV_CH:])
    u_ext = jnp.concatenate([u_hist, u], axis=1)
    dw = lax.conv_general_dilated(u_ext, conv_w[:, None, :], (1,), 'VALID',
                                  dimension_numbers=('NWC', 'WIO', 'NWC'),
                                  feature_group_count=CONV_CH) + conv_b
    conv_branch = jax.nn.silu(layernorm(dw, conv_ln_g, conv_ln_b)) @ w_conv_out

    gate_a = jax.nn.sigmoid(g[..., :D_MODEL])
    gate_c = jax.nn.sigmoid(g[..., D_MODEL:])
    x = x + (gate_a * attn_branch + gate_c * conv_branch) @ w_out

    x = x + jnp.square(jax.nn.relu(rmsnorm(x, norm_ffn_g) @ w_ff_up)) @ w_ff_down

    x = x + jax.nn.sigmoid(rmsnorm(x, ple_norm_g) @ w_ple_gate) * (p_emb @ w_ple_proj)
    return x, latent, k_rope, u_ext[:, -(CONV_W - 1):]


def setup_inputs(seed: int = 0) -> dict:
    key = jax.random.key(seed)
    ks = jax.random.split(key, 32)
    f32 = jnp.float32
    L = DEPTH

    def nrm(k, shape, scale):
        return jax.random.normal(k, shape, f32) * scale

    def gain(k, n):
        return 1.0 + 0.1 * jax.random.normal(k, (L, n), f32)

    return {
        'x_prompt': nrm(ks[0], (BATCH, SEQ, D_MODEL), 1.0),
        'x_sample': nrm(ks[1], (DEC_BATCH, DEC_SEQ, D_MODEL), 1.0),
        'p_prompt': nrm(ks[2], (L, BATCH, SEQ, PLE_DIM), 1.0),
        'p_sample': nrm(ks[3], (L, DEC_BATCH, DEC_SEQ, PLE_DIM), 1.0),
        'cache_kv_latent': nrm(ks[4], (L, DEC_BATCH, PAST_LEN, KV_LORA), 1.0),
        'cache_k_rope': nrm(ks[5], (L, DEC_BATCH, PAST_LEN, QK_ROPE), 1.0),
        'cache_conv': nrm(ks[6], (L, DEC_BATCH, CONV_W - 1, CONV_CH), 0.5),
        'norm_mix_g': gain(ks[7], D_MODEL),
        'w_in': nrm(ks[8], (L, D_MODEL, IN_W), D_MODEL ** -0.5),
        'q_norm_g': gain(ks[9], Q_LORA),
        'w_uq': nrm(ks[10], (L, Q_LORA, N_HEADS, QK_NOPE + QK_ROPE), Q_LORA ** -0.5),
        'kv_norm_g': gain(ks[11], KV_LORA),
        'w_uk': nrm(ks[12], (L, KV_LORA, N_HEADS, QK_NOPE), KV_LORA ** -0.5),
        'w_uv': nrm(ks[13], (L, KV_LORA, N_HEADS, V_DIM), KV_LORA ** -0.5),
        'w_attn_out': nrm(ks[14], (L, N_HEADS * V_DIM, D_MODEL), (N_HEADS * V_DIM) ** -0.5),
        'conv_w': nrm(ks[15], (L, CONV_W, CONV_CH), CONV_W ** -0.5),
        'conv_b': nrm(ks[16], (L, CONV_CH), 0.01),
        'conv_ln_g': gain(ks[17], CONV_CH),
        'conv_ln_b': nrm(ks[18], (L, CONV_CH), 0.01),
        'w_conv_out': nrm(ks[19], (L, CONV_CH, D_MODEL), CONV_CH ** -0.5),
        'w_out': nrm(ks[20], (L, D_MODEL, D_MODEL), D_MODEL ** -0.5),
        'norm_ffn_g': gain(ks[21], D_MODEL),
        'w_ff_up': nrm(ks[22], (L, D_MODEL, D_FF), D_MODEL ** -0.5),
        'w_ff_down': nrm(ks[23], (L, D_FF, D_MODEL), D_FF ** -0.5),
        'ple_norm_g': gain(ks[24], D_MODEL),
        'w_ple_gate': nrm(ks[25], (L, D_MODEL, D_MODEL), D_MODEL ** -0.5),
        'w_ple_proj': nrm(ks[26], (L, PLE_DIM, D_MODEL), PLE_DIM ** -0.5),
        'final_norm_g': 1.0 + 0.1 * jax.random.normal(ks[27], (D_MODEL,), f32),
    }


def reference(x_prompt, x_sample, p_prompt, p_sample, cache_kv_latent, cache_k_rope, cache_conv,
              norm_mix_g, w_in, q_norm_g, w_uq, kv_norm_g, w_uk, w_uv, w_attn_out,
              conv_w, conv_b, conv_ln_g, conv_ln_b, w_conv_out, w_out,
              norm_ffn_g, w_ff_up, w_ff_down, ple_norm_g, w_ple_gate, w_ple_proj, final_norm_g):
    t_prompt = x_prompt.shape[1]
    t_sample = x_sample.shape[1]
    past = cache_kv_latent.shape[2]
    pos_p = jnp.arange(t_prompt)
    pos_s = past + jnp.arange(t_sample)
    xp, xs = x_prompt, x_sample
    lat_p, kr_p, cv_p, lat_s, kr_s, cv_s = [], [], [], [], [], []
    for i in range(DEPTH):
        lw = (norm_mix_g[i], w_in[i], q_norm_g[i], w_uq[i], kv_norm_g[i], w_uk[i], w_uv[i],
              w_attn_out[i], conv_w[i], conv_b[i], conv_ln_g[i], conv_ln_b[i], w_conv_out[i],
              w_out[i], norm_ffn_g[i], w_ff_up[i], w_ff_down[i], ple_norm_g[i], w_ple_gate[i],
              w_ple_proj[i])
        xp, a, b, c = encoder_layer(xp, p_prompt[i], pos_p, None, None, None, *lw)
        lat_p.append(a)
        kr_p.append(b)
        cv_p.append(c)
        xs, a, b, c = encoder_layer(xs, p_sample[i], pos_s, cache_kv_latent[i], cache_k_rope[i],
                                    cache_conv[i], *lw)
        lat_s.append(a)
        kr_s.append(b)
        cv_s.append(c)
    y_prompt = rmsnorm(xp, final_norm_g)
    y_sample = rmsnorm(xs, final_norm_g)
    return (y_prompt, y_sample, jnp.stack(lat_p), jnp.stack(kr_p), jnp.stack(cv_p),
            jnp.stack(lat_s), jnp.stack(kr_s), jnp.stack(cv_s))
```

```python
import functools

import jax
import jax.numpy as jnp
from jax import lax
from jax.experimental import pallas as pl
from jax.experimental.pallas import tpu as pltpu

F32 = jnp.float32
BF16 = jnp.bfloat16

CHUNK = 64
N_HEADS = 8
QK_NOPE = 64
QK_ROPE = 32
V_DIM = 64
CONV_W = 31
ROPE_THETA = 10000.0
RMS_EPS = 1e-6
LN_EPS = 1e-5
SM_SCALE = (QK_NOPE + QK_ROPE) ** -0.5
NEG_INF = -1e30

LANES = 128
HEAD_SLAB = LANES
HIST_ROWS = 32
HIST_PAD = HIST_ROWS - (CONV_W - 1)
VMEM_LIMIT = 56 * 1024 * 1024


def _rms(x, g):
    return x * lax.rsqrt(jnp.mean(x * x, axis=-1, keepdims=True) + RMS_EPS) * g


def _dot(a, b):
    return jnp.dot(a, b, preferred_element_type=F32)


def _const_spec(shape):
    nd = len(shape)
    return pl.BlockSpec(shape, lambda *_: (0,) * nd, pipeline_mode=pl.Buffered(1))


def _stage_in_body(x_ref, hist_ref, cs_ref, gmix_ref, wa_ref, gq_ref, gkv_ref, wq_ref, wqr_ref,
                   *rest, nb, tt, absorb, c_q, c_kv, conv_ch):
    if absorb:
        wukt_ref, bconv_ref, wconv_ref, lng_ref, lnb_ref = rest[:5]
        outs = rest[5:]
        q_ref, qlat_ref, lat_ref, kr_ref, cact_ref, ncv_ref, uext_ref = outs
    else:
        wkn_ref, wv_ref, bconv_ref, wconv_ref, lng_ref, lnb_ref = rest[:6]
        outs = rest[6:]
        q_ref, k_ref, v_ref, lat_ref, kr_ref, cact_ref, ncv_ref, uext_ref = outs

    t = pl.program_id(1)
    rows = nb * tt
    d = x_ref.shape[-1]

    x = x_ref[...].reshape(rows, d)
    h = _rms(x, gmix_ref[...]).astype(BF16)
    z = _dot(h, wa_ref[...])
    o_kv = c_q
    o_ks = o_kv + c_kv
    o_kr = o_ks + HEAD_SLAB
    o_u = o_kr + HEAD_SLAB

    cos = cs_ref[:, :HEAD_SLAB]
    sin = cs_ref[:, HEAD_SLAB:]

    def rope3(a, a_rot):
        a3 = a.reshape(nb, tt, HEAD_SLAB) * cos[None] + a_rot.reshape(nb, tt, HEAD_SLAB) * sin[None]
        return a3.reshape(rows, HEAD_SLAB)

    cqn = _rms(z[:, :c_q], gq_ref[...]).astype(BF16)
    qa = _dot(cqn, wq_ref[...])
    qb = _dot(cqn, wqr_ref[...])
    q_heads = []
    for hd in range(N_HEADS):
        sl = slice(hd * HEAD_SLAB, (hd + 1) * HEAD_SLAB)
        qh = rope3(qa[:, sl], qb[:, sl]).astype(BF16)
        q_heads.append(qh)
        q_ref[:, :, sl] = qh.reshape(nb, tt, HEAD_SLAB)

    latent = _rms(z[:, o_kv:o_ks], gkv_ref[...])
    lat_ref[...] = latent.reshape(nb, tt, c_kv)
    kslab = rope3(z[:, o_ks:o_kr], z[:, o_kr:o_u])
    kr_ref[...] = kslab[:, QK_NOPE:QK_NOPE + QK_ROPE].reshape(nb, tt, QK_ROPE)

    if absorb:
        for hd in range(N_HEADS):
            ql = _dot(q_heads[hd], wukt_ref[hd])
            qlat_ref[:, :, hd * c_kv:(hd + 1) * c_kv] = ql.astype(BF16).reshape(nb, tt, c_kv)
    else:
        lat_b = latent.astype(BF16)
        kn = _dot(lat_b, wkn_ref[...])
        for hd in range(N_HEADS):
            sl = slice(hd * HEAD_SLAB, (hd + 1) * HEAD_SLAB)
            k_ref[:, :, sl] = (kn[:, sl] + kslab).astype(BF16).reshape(nb, tt, HEAD_SLAB)
        v_ref[...] = _dot(lat_b, wv_ref[...]).astype(BF16).reshape(nb, tt, v_ref.shape[-1])

    u = z[:, o_u:o_u + conv_ch] * jax.nn.sigmoid(z[:, o_u + conv_ch:o_u + 2 * conv_ch])

    @pl.when(t == 0)
    def _():
        uext_ref[:, :HIST_ROWS, :] = hist_ref[...]

    uext_ref[:, HIST_ROWS:, :] = u.reshape(nb, tt, conv_ch)
    dw = jnp.zeros((nb, tt, conv_ch), F32) + bconv_ref[...][None]
    for k in range(CONV_W):
        dw = dw + uext_ref[:, HIST_PAD + k:HIST_PAD + k + tt, :] * wconv_ref[k:k + 1, :][None]
    mu = jnp.mean(dw, axis=-1, keepdims=True)
    dc = dw - mu
    ln = dc * lax.rsqrt(jnp.mean(dc * dc, axis=-1, keepdims=True) + LN_EPS)
    ln = ln * lng_ref[...][None] + lnb_ref[...][None]
    cact_ref[...] = (ln * jax.nn.sigmoid(ln)).astype(BF16)

    @pl.when(t == pl.num_programs(1) - 1)
    def _():
        ncv_ref[...] = uext_ref[:, tt + HIST_PAD:tt + HIST_ROWS, :]

    uext_ref[:, :HIST_ROWS, :] = uext_ref[:, tt:tt + HIST_ROWS, :]


def _stage_in(x, hist, cs, wts, *, nb, tt, absorb):
    b, t_len, d = x.shape
    c_q = wts["gq"].shape[-1]
    c_kv = wts["gkv"].shape[-1]
    conv_ch = wts["bconv"].shape[-1]
    grid = (b // nb, t_len // tt)

    def row_spec(width):
        return pl.BlockSpec((nb, tt, width), lambda i, j: (i, j, 0))

    in_specs = [
        row_spec(d),
        pl.BlockSpec((nb, HIST_ROWS, conv_ch), lambda i, j: (i, 0, 0)),
        pl.BlockSpec((tt, 2 * HEAD_SLAB), lambda i, j: (j, 0)),
    ]
    names = ["gmix", "wa", "gq", "gkv", "wq", "wqr"]
    names += ["wukt"] if absorb else ["wkn", "wv"]
    names += ["bconv", "wconv", "lng", "lnb"]
    in_specs += [_const_spec(wts[n].shape) for n in names]

    qk_w = N_HEADS * HEAD_SLAB
    out_shapes = [jax.ShapeDtypeStruct((b, t_len, qk_w), BF16)]
    out_specs = [row_spec(qk_w)]
    if absorb:
        out_shapes.append(jax.ShapeDtypeStruct((b, t_len, N_HEADS * c_kv), BF16))
        out_specs.append(row_spec(N_HEADS * c_kv))
    else:
        out_shapes += [jax.ShapeDtypeStruct((b, t_len, qk_w), BF16),
                       jax.ShapeDtypeStruct((b, t_len, N_HEADS * V_DIM), BF16)]
        out_specs += [row_spec(qk_w), row_spec(N_HEADS * V_DIM)]
    out_shapes += [jax.ShapeDtypeStruct((b, t_len, c_kv), F32),
                   jax.ShapeDtypeStruct((b, t_len, QK_ROPE), F32),
                   jax.ShapeDtypeStruct((b, t_len, conv_ch), BF16),
                   jax.ShapeDtypeStruct((b, CONV_W - 1, conv_ch), F32)]
    out_specs += [row_spec(c_kv), row_spec(QK_ROPE), row_spec(conv_ch),
                  pl.BlockSpec((nb, CONV_W - 1, conv_ch), lambda i, j: (i, 0, 0))]

    body = functools.partial(_stage_in_body, nb=nb, tt=tt, absorb=absorb, c_q=c_q, c_kv=c_kv,
                             conv_ch=conv_ch)
    return pl.pallas_call(
        body,
        grid=grid,
        in_specs=in_specs,
        out_specs=out_specs,
        out_shape=out_shapes,
        scratch_shapes=[pltpu.VMEM((nb, tt + HIST_ROWS, conv_ch), F32)],
        compiler_params=pltpu.CompilerParams(dimension_semantics=("arbitrary", "arbitrary"),
                                             vmem_limit_bytes=VMEM_LIMIT),
        name="stage_in_absorb" if absorb else "stage_in",
    )(x, hist, cs, *[wts[n] for n in names])


def _attn_prompt_body(q_ref, k_ref, v_ref, o_ref, *, tq):
    qi = pl.program_id(2)
    row_chunk = lax.broadcasted_iota(jnp.int32, (tq, tq), 0) // CHUNK
    col_chunk = lax.broadcasted_iota(jnp.int32, (tq, tq), 1) // CHUNK
    diag_mask = col_chunk <= row_chunk

    outs = []
    for hh in range(2):
        sl = slice(hh * HEAD_SLAB, (hh + 1) * HEAD_SLAB)
        q = q_ref[0, :, sl]

        def step(kt, carry, masked, q=q, sl=sl):
            m, l, acc = carry
            start = pl.multiple_of(kt * tq, tq)
            k = k_ref[0, pl.ds(start, tq), sl]
            v = v_ref[0, pl.ds(start, tq), :]
            s = lax.dot_general(q, k, (((1,), (1,)), ((), ())), preferred_element_type=F32) * SM_SCALE
            if masked:
                s = jnp.where(diag_mask, s, NEG_INF)
            m_new = jnp.maximum(m, jnp.max(s, axis=-1, keepdims=True))
            alpha = jnp.exp(m - m_new)
            p = jnp.exp(s - m_new)
            l = alpha * l + jnp.sum(p, axis=-1, keepdims=True)
            acc = alpha * acc + _dot(p.astype(BF16), v)
            return m_new, l, acc

        init = (jnp.full((tq, 1), NEG_INF, F32), jnp.zeros((tq, 1), F32),
                jnp.zeros((tq, 2 * V_DIM), F32))
        carry = lax.fori_loop(0, qi, functools.partial(step, masked=False), init)
        _, l, acc = step(qi, carry, True)
        outs.append(acc / l)

    lane = lax.broadcasted_iota(jnp.int32, (tq, 2 * V_DIM), 1)
    o_ref[0] = jnp.where(lane < V_DIM, outs[0], outs[1]).astype(BF16)


def _attn_prompt(q, k, v, *, tq):
    b, t_len, _ = q.shape
    grid = (b, N_HEADS // 2, t_len // tq)
    return pl.pallas_call(
        functools.partial(_attn_prompt_body, tq=tq),
        grid=grid,
        in_specs=[pl.BlockSpec((1, tq, 2 * HEAD_SLAB), lambda i, p, j: (i, j, p)),
                  pl.BlockSpec((1, t_len, 2 * HEAD_SLAB), lambda i, p, j: (i, 0, p)),
                  pl.BlockSpec((1, t_len, 2 * V_DIM), lambda i, p, j: (i, 0, p))],
        out_specs=pl.BlockSpec((1, tq, 2 * V_DIM), lambda i, p, j: (i, j, p)),
        out_shape=jax.ShapeDtypeStruct((b, t_len, N_HEADS * V_DIM), BF16),
        compiler_params=pltpu.CompilerParams(dimension_semantics=("arbitrary",) * 3,
                                             vmem_limit_bytes=VMEM_LIMIT),
        name="attn_prompt",
    )(q, k, v)


def _attn_sample_body(q_ref, qlat_ref, clat_ref, ckr_ref, nlat_ref, nkr_ref, o_ref, *, past, tt, c_kv):
    ql = jnp.concatenate([qlat_ref[0, :, hd * c_kv:(hd + 1) * c_kv] for hd in range(N_HEADS)], axis=0)
    qs = jnp.concatenate([q_ref[0, :, hd * HEAD_SLAB:(hd + 1) * HEAD_SLAB] for hd in range(N_HEADS)],
                         axis=0)
    qr = qs[:, QK_NOPE:QK_NOPE + QK_ROPE]
    nt = (((1,), (1,)), ((), ()))

    clat = clat_ref[0].astype(BF16)
    ckr = ckr_ref[0].astype(BF16)
    nlat = nlat_ref[0].astype(BF16)
    nkr = nkr_ref[0].astype(BF16)
    s_c = (lax.dot_general(ql, clat, nt, preferred_element_type=F32)
           + lax.dot_general(qr, ckr, nt, preferred_element_type=F32)) * SM_SCALE
    s_n = (lax.dot_general(ql, nlat, nt, preferred_element_type=F32)
           + lax.dot_general(qr, nkr, nt, preferred_element_type=F32)) * SM_SCALE
    rows = N_HEADS * tt
    q_pos = past + lax.broadcasted_iota(jnp.int32, (rows, tt), 0) % tt
    k_pos = past + lax.broadcasted_iota(jnp.int32, (rows, tt), 1)
    s_n = jnp.where(k_pos // CHUNK <= q_pos // CHUNK, s_n, NEG_INF)

    m = jnp.maximum(jnp.max(s_c, axis=-1, keepdims=True), jnp.max(s_n, axis=-1, keepdims=True))
    p_c = jnp.exp(s_c - m)
    p_n = jnp.exp(s_n - m)
    l = jnp.sum(p_c, axis=-1, keepdims=True) + jnp.sum(p_n, axis=-1, keepdims=True)
    o = (_dot(p_c.astype(BF16), clat) + _dot(p_n.astype(BF16), nlat)) / l
    for hd in range(N_HEADS):
        o_ref[0, :, hd * c_kv:(hd + 1) * c_kv] = o[hd * tt:(hd + 1) * tt, :].astype(BF16)


def _attn_sample(q, qlat, clat, ckr, nlat, nkr):
    b, tt, _ = q.shape
    past, c_kv = clat.shape[1], clat.shape[2]

    def spec(a):
        return pl.BlockSpec((1,) + a.shape[1:], lambda i: (i, 0, 0))

    return pl.pallas_call(
        functools.partial(_attn_sample_body, past=past, tt=tt, c_kv=c_kv),
        grid=(b,),
        in_specs=[spec(a) for a in (q, qlat, clat, ckr, nlat, nkr)],
        out_specs=pl.BlockSpec((1, tt, N_HEADS * c_kv), lambda i: (i, 0, 0)),
        out_shape=jax.ShapeDtypeStruct((b, tt, N_HEADS * c_kv), BF16),
        compiler_params=pltpu.CompilerParams(dimension_semantics=("arbitrary",),
                                             vmem_limit_bytes=VMEM_LIMIT),
        name="attn_sample",
    )(q, qlat, clat, ckr, nlat, nkr)


def _stage_out_body(x_ref, o_ref, c_ref, p_ref, gmix_ref, wg_ref, wao_ref, wco_ref, wout_ref, gffn_ref,
                    wup_ref, wdn_ref, gple_ref, wpg_ref, wpp_ref, gfin_ref, *rest, absorb, ff_chunk):
    if absorb:
        wuv_ref, y_ref = rest
    else:
        (y_ref,) = rest
    x = x_ref[...]
    d = x.shape[-1]
    h = _rms(x, gmix_ref[...]).astype(BF16)
    g = _dot(h, wg_ref[...])
    o = o_ref[...]
    if absorb:
        o = _dot(o, wuv_ref[...]).astype(BF16)
    a = _dot(o, wao_ref[...])
    cb = _dot(c_ref[...], wco_ref[...])
    m = jax.nn.sigmoid(g[:, :d]) * a + jax.nn.sigmoid(g[:, d:]) * cb
    x = x + _dot(m.astype(BF16), wout_ref[...])

    h = _rms(x, gffn_ref[...]).astype(BF16)
    d_ff = wup_ref.shape[-1]
    ff = jnp.zeros_like(x)
    for c in range(d_ff // ff_chunk):
        sl = slice(c * ff_chunk, (c + 1) * ff_chunk)
        up = jnp.maximum(_dot(h, wup_ref[:, sl]), 0.0)
        ff = ff + _dot((up * up).astype(BF16), wdn_ref[sl, :])
    x = x + ff

    h = _rms(x, gple_ref[...]).astype(BF16)
    pg = jax.nn.sigmoid(_dot(h, wpg_ref[...]))
    x = x + pg * _dot(p_ref[...].astype(BF16), wpp_ref[...])
    y_ref[...] = _rms(x, gfin_ref[...])


def _stage_out(x, o, cact, p, wts, *, tm, absorb):
    n, d = x.shape
    names = ["gmix", "wg", "wao", "wco", "wout", "gffn", "wup", "wdn", "gple", "wpg", "wpp", "gfin"]
    if absorb:
        names.append("wuv_bd")

    def row_spec(a):
        return pl.BlockSpec((tm, a.shape[-1]), lambda i: (i, 0))

    return pl.pallas_call(
        functools.partial(_stage_out_body, absorb=absorb, ff_chunk=1024),
        grid=(n // tm,),
        in_specs=[row_spec(a) for a in (x, o, cact, p)] + [_const_spec(wts[k].shape) for k in names],
        out_specs=pl.BlockSpec((tm, d), lambda i: (i, 0)),
        out_shape=jax.ShapeDtypeStruct((n, d), F32),
        compiler_params=pltpu.CompilerParams(dimension_semantics=("arbitrary",),
                                             vmem_limit_bytes=VMEM_LIMIT),
        name="stage_out_absorb" if absorb else "stage_out",
    )(x, o, cact, p, *[wts[k] for k in names])


def _rot_half_cols(w):
    half = w.shape[-1] // 2
    return jnp.concatenate([-w[..., half:], w[..., :half]], axis=-1)


def _head_slabs(nope, rope):
    c, hds = (nope if nope is not None else rope).shape[:2]
    parts = [nope if nope is not None else jnp.zeros((c, hds, QK_NOPE), F32),
             rope if rope is not None else jnp.zeros((c, hds, QK_ROPE), F32),
             jnp.zeros((c, hds, HEAD_SLAB - QK_NOPE - QK_ROPE), F32)]
    return jnp.concatenate(parts, axis=-1).reshape(c, hds * HEAD_SLAB)


def _prep_weights(norm_mix_g, w_in, q_norm_g, w_uq, kv_norm_g, w_uk, w_uv, w_attn_out, conv_w, conv_b,
                  conv_ln_g, conv_ln_b, w_conv_out, w_out, norm_ffn_g, w_ff_up, w_ff_down, ple_norm_g,
                  w_ple_gate, w_ple_proj, final_norm_g):
    d = w_in.shape[0]
    c_q = q_norm_g.shape[-1]
    c_kv = kv_norm_g.shape[-1]
    conv_ch = conv_b.shape[-1]
    o_kv = c_q
    o_kr = o_kv + c_kv
    o_conv = o_kr + QK_ROPE
    o_gate = o_conv + 2 * conv_ch
    w_kr = w_in[:, o_kr:o_conv]
    zn = jnp.zeros((d, QK_NOPE), F32)
    zp = jnp.zeros((d, HEAD_SLAB - QK_NOPE - QK_ROPE), F32)
    wa = jnp.concatenate([w_in[:, :o_kr], zn, w_kr, zp, zn, _rot_half_cols(w_kr), zp,
                          w_in[:, o_conv:o_gate]], axis=1)
    q_nope, q_rope = w_uq[..., :QK_NOPE], w_uq[..., QK_NOPE:]
    row = lambda v: v.reshape(1, -1).astype(F32)
    wukt = jnp.transpose(w_uk, (1, 2, 0))
    wukt = jnp.concatenate([wukt, jnp.zeros((N_HEADS, HEAD_SLAB - QK_NOPE, c_kv), F32)], axis=1)
    eye = jnp.eye(N_HEADS, dtype=F32)
    wuv_bd = jnp.einsum("chd,hg->hcgd", w_uv, eye).reshape(N_HEADS * c_kv, N_HEADS * V_DIM)
    return {
        "gmix": row(norm_mix_g), "wa": wa.astype(BF16), "gq": row(q_norm_g), "gkv": row(kv_norm_g),
        "wq": _head_slabs(q_nope, q_rope).astype(BF16),
        "wqr": _head_slabs(None, _rot_half_cols(q_rope)).astype(BF16),
        "wkn": _head_slabs(w_uk, None).astype(BF16),
        "wv": w_uv.reshape(c_kv, N_HEADS * V_DIM).astype(BF16),
        "wukt": wukt.astype(BF16), "wuv_bd": wuv_bd.astype(BF16),
        "bconv": row(conv_b), "wconv": conv_w.astype(F32), "lng": row(conv_ln_g), "lnb": row(conv_ln_b),
        "wg": w_in[:, o_gate:].astype(BF16), "wao": w_attn_out.astype(BF16),
        "wco": w_conv_out.astype(BF16), "wout": w_out.astype(BF16), "gffn": row(norm_ffn_g),
        "wup": w_ff_up.astype(BF16), "wdn": w_ff_down.astype(BF16), "gple": row(ple_norm_g),
        "wpg": w_ple_gate.astype(BF16), "wpp": w_ple_proj.astype(BF16), "gfin": row(final_norm_g),
    }


def _rope_tables(pos):
    half = QK_ROPE // 2
    inv = ROPE_THETA ** (-jnp.arange(half, dtype=F32) / half)
    ang = pos.astype(F32)[:, None] * inv[None, :]
    cos, sin = jnp.cos(ang), jnp.sin(ang)
    n = pos.shape[0]
    pad = jnp.zeros((n, HEAD_SLAB - QK_NOPE - QK_ROPE), F32)
    return jnp.concatenate([jnp.ones((n, QK_NOPE), F32), cos, cos, pad,
                            jnp.zeros((n, QK_NOPE), F32), sin, sin, pad], axis=1)


def _tile(n, pref):
    return pref if n % pref == 0 else n


def kernel(x_prompt, x_sample, p_prompt, p_sample, cache_kv_latent, cache_k_rope, cache_conv, norm_mix_g, w_in, q_norm_g, w_uq, kv_norm_g, w_uk, w_uv, w_attn_out, conv_w, conv_b, conv_ln_g, conv_ln_b, w_conv_out, w_out, norm_ffn_g, w_ff_up, w_ff_down, ple_norm_g, w_ple_gate, w_ple_proj, final_norm_g):
    depth = w_in.shape[0]
    assert depth == 1, "one layer: the two request groups are independent within it"
    bp, tp, d = x_prompt.shape
    bs, ts, _ = x_sample.shape
    past = cache_kv_latent.shape[2]
    conv_ch = conv_b.shape[-1]
    wts = _prep_weights(norm_mix_g[0], w_in[0], q_norm_g[0], w_uq[0], kv_norm_g[0], w_uk[0], w_uv[0],
                        w_attn_out[0], conv_w[0], conv_b[0], conv_ln_g[0], conv_ln_b[0], w_conv_out[0],
                        w_out[0], norm_ffn_g[0], w_ff_up[0], w_ff_down[0], ple_norm_g[0], w_ple_gate[0],
                        w_ple_proj[0], final_norm_g)

    tt = _tile(tp, 512)
    hist0 = jnp.zeros((bp, HIST_ROWS, conv_ch), F32)
    q, k, v, lat_p, kr_p, cact_p, ncv_p = _stage_in(x_prompt, hist0, _rope_tables(jnp.arange(tp)), wts,
                                                     nb=1, tt=tt, absorb=False)
    o_p = _attn_prompt(q, k, v, tq=_tile(tp, 256))
    n_p = bp * tp
    y_p = _stage_out(x_prompt.reshape(n_p, d), o_p.reshape(n_p, -1), cact_p.reshape(n_p, -1),
                     p_prompt[0].reshape(n_p, -1), wts, tm=_tile(n_p, 512), absorb=False)

    hist_s = jnp.pad(cache_conv[0], ((0, 0), (HIST_PAD, 0), (0, 0)))
    qs, qlat, lat_s, kr_s, cact_s, ncv_s = _stage_in(x_sample, hist_s,
                                                     _rope_tables(past + jnp.arange(ts)), wts,
                                                     nb=bs, tt=ts, absorb=True)
    o_s = _attn_sample(qs, qlat, cache_kv_latent[0], cache_k_rope[0], lat_s, kr_s)
    n_s = bs * ts
    y_s = _stage_out(x_sample.reshape(n_s, d), o_s.reshape(n_s, -1), cact_s.reshape(n_s, -1),
                     p_sample[0].reshape(n_s, -1), wts, tm=_tile(n_s, 512), absorb=True)

    return (y_p.reshape(bp, tp, d), y_s.reshape(bs, ts, d), lat_p[None], kr_p[None], ncv_p[None],
            lat_s[None], kr_s[None], ncv_s[None])
```

```python
import functools
import math

import jax
import jax.numpy as jnp
from jax import lax
from jax.experimental import pallas as pl
from jax.experimental.pallas import tpu as pltpu

F32 = jnp.float32
BF16 = jnp.bfloat16

CHUNK = 64
N_HEADS = 8
QK_NOPE = 64
QK_ROPE = 32
V_DIM = 64
CONV_W = 31
ROPE_THETA = 10000.0
RMS_EPS = 1e-6
LN_EPS = 1e-5
SM_SCALE = (QK_NOPE + QK_ROPE) ** -0.5
EXP2_SCALE = SM_SCALE * math.log2(math.e)
NEG_INF = -1e30

LANES = 128
HEAD_SLAB = LANES
HIST_ROWS = 32
HIST_PAD = HIST_ROWS - (CONV_W - 1)
ATTN_TQ = 512
ATTN_QC = 256
VMEM_LIMIT = 56 * 1024 * 1024


def _rms(x, g):
    return x * lax.rsqrt(jnp.mean(x * x, axis=-1, keepdims=True) + RMS_EPS) * g


def _dot(a, b):
    return jnp.dot(a, b, preferred_element_type=F32)


NT_DIMS = (((1,), (1,)), ((), ()))
TN_DIMS = (((0,), (0,)), ((), ()))


def _const_spec(shape):
    nd = len(shape)
    return pl.BlockSpec(shape, lambda *_: (0,) * nd, pipeline_mode=pl.Buffered(1))


def _stage_in_body(x_ref, hist_ref, cs_ref, gmix_ref, wa_ref, gq_ref, gkv_ref, wq_ref, wqr_ref,
                   *rest, nb, tt, absorb, c_q, c_kv, conv_ch):
    if absorb:
        wukt_ref, bconv_ref, wconv_ref, lng_ref, lnb_ref = rest[:5]
        outs = rest[5:]
        q_ref, qlat_ref, lat_ref, kr_ref, cact_ref, ncv_ref, uext_ref = outs
    else:
        wkn_ref, wv_ref, bconv_ref, wconv_ref, lng_ref, lnb_ref = rest[:6]
        outs = rest[6:]
        q_ref, k_ref, v_ref, lat_ref, kr_ref, cact_ref, ncv_ref, uext_ref = outs

    t = pl.program_id(1)
    rows = nb * tt
    d = x_ref.shape[-1]

    x = x_ref[...].reshape(rows, d)
    h = _rms(x, gmix_ref[...]).astype(BF16)
    z = _dot(h, wa_ref[...])
    o_kv = c_q
    o_ks = o_kv + c_kv
    o_kr = o_ks + HEAD_SLAB
    o_u = o_kr + HEAD_SLAB

    cos = cs_ref[:, :HEAD_SLAB]
    sin = cs_ref[:, HEAD_SLAB:]

    def rope3(a, a_rot):
        a3 = a.reshape(nb, tt, HEAD_SLAB) * cos[None] + a_rot.reshape(nb, tt, HEAD_SLAB) * sin[None]
        return a3.reshape(rows, HEAD_SLAB)

    cqn = _rms(z[:, :c_q], gq_ref[...]).astype(BF16)
    qa = _dot(cqn, wq_ref[...])
    qb = _dot(cqn, wqr_ref[...])
    q_heads = []
    for hd in range(N_HEADS):
        sl = slice(hd * HEAD_SLAB, (hd + 1) * HEAD_SLAB)
        qh = rope3(qa[:, sl], qb[:, sl]).astype(BF16)
        q_heads.append(qh)
        q_ref[:, :, sl] = qh.reshape(nb, tt, HEAD_SLAB)

    latent = _rms(z[:, o_kv:o_ks], gkv_ref[...])
    lat_ref[...] = latent.reshape(nb, tt, c_kv)
    kslab = rope3(z[:, o_ks:o_kr], z[:, o_kr:o_u])
    kr_ref[...] = kslab[:, QK_NOPE:QK_NOPE + QK_ROPE].reshape(nb, tt, QK_ROPE)

    if absorb:
        for hd in range(N_HEADS):
            ql = _dot(q_heads[hd], wukt_ref[hd])
            qlat_ref[:, :, hd * c_kv:(hd + 1) * c_kv] = ql.astype(BF16).reshape(nb, tt, c_kv)
    else:
        lat_b = latent.astype(BF16)
        kn = _dot(lat_b, wkn_ref[...])
        for hd in range(N_HEADS):
            sl = slice(hd * HEAD_SLAB, (hd + 1) * HEAD_SLAB)
            k_ref[:, :, sl] = (kn[:, sl] + kslab).astype(BF16).reshape(nb, tt, HEAD_SLAB)
        v_ref[0] = lax.dot_general(wv_ref[...], lat_b, NT_DIMS, preferred_element_type=F32).astype(BF16)

    u = z[:, o_u:o_u + conv_ch] * jax.nn.sigmoid(z[:, o_u + conv_ch:o_u + 2 * conv_ch])

    @pl.when(t == 0)
    def _():
        uext_ref[:, :HIST_ROWS, :] = hist_ref[...]

    uext_ref[:, HIST_ROWS:, :] = u.reshape(nb, tt, conv_ch)
    dw = jnp.zeros((nb, tt, conv_ch), F32) + bconv_ref[...][None]
    for k in range(CONV_W):
        dw = dw + uext_ref[:, HIST_PAD + k:HIST_PAD + k + tt, :] * wconv_ref[k:k + 1, :][None]
    mu = jnp.mean(dw, axis=-1, keepdims=True)
    dc = dw - mu
    ln = dc * lax.rsqrt(jnp.mean(dc * dc, axis=-1, keepdims=True) + LN_EPS)
    ln = ln * lng_ref[...][None] + lnb_ref[...][None]
    cact_ref[...] = (ln * jax.nn.sigmoid(ln)).astype(BF16)

    @pl.when(t == pl.num_programs(1) - 1)
    def _():
        ncv_ref[...] = uext_ref[:, tt + HIST_PAD:tt + HIST_ROWS, :]

    uext_ref[:, :HIST_ROWS, :] = uext_ref[:, tt:tt + HIST_ROWS, :]


def _stage_in(x, hist, cs, wts, *, nb, tt, absorb):
    b, t_len, d = x.shape
    c_q = wts["gq"].shape[-1]
    c_kv = wts["gkv"].shape[-1]
    conv_ch = wts["bconv"].shape[-1]
    grid = (b // nb, t_len // tt)

    def row_spec(width):
        return pl.BlockSpec((nb, tt, width), lambda i, j: (i, j, 0))

    in_specs = [
        row_spec(d),
        pl.BlockSpec((nb, HIST_ROWS, conv_ch), lambda i, j: (i, 0, 0)),
        pl.BlockSpec((tt, 2 * HEAD_SLAB), lambda i, j: (j, 0)),
    ]
    names = ["gmix", "wa", "gq", "gkv", "wq", "wqr"]
    names += ["wukt"] if absorb else ["wkn", "wv"]
    names += ["bconv", "wconv", "lng", "lnb"]
    in_specs += [_const_spec(wts[n].shape) for n in names]

    qk_w = N_HEADS * HEAD_SLAB
    out_shapes = [jax.ShapeDtypeStruct((b, t_len, qk_w), BF16)]
    out_specs = [row_spec(qk_w)]
    if absorb:
        out_shapes.append(jax.ShapeDtypeStruct((b, t_len, N_HEADS * c_kv), BF16))
        out_specs.append(row_spec(N_HEADS * c_kv))
    else:
        assert nb == 1, "transposed value tiles are written one sequence at a time"
        out_shapes += [jax.ShapeDtypeStruct((b, t_len, qk_w), BF16),
                       jax.ShapeDtypeStruct((b, N_HEADS * V_DIM, t_len), BF16)]
        out_specs += [row_spec(qk_w), pl.BlockSpec((1, N_HEADS * V_DIM, tt), lambda i, j: (i, 0, j))]
    out_shapes += [jax.ShapeDtypeStruct((b, t_len, c_kv), F32),
                   jax.ShapeDtypeStruct((b, t_len, QK_ROPE), F32),
                   jax.ShapeDtypeStruct((b, t_len, conv_ch), BF16),
                   jax.ShapeDtypeStruct((b, CONV_W - 1, conv_ch), F32)]
    out_specs += [row_spec(c_kv), row_spec(QK_ROPE), row_spec(conv_ch),
                  pl.BlockSpec((nb, CONV_W - 1, conv_ch), lambda i, j: (i, 0, 0))]

    body = functools.partial(_stage_in_body, nb=nb, tt=tt, absorb=absorb, c_q=c_q, c_kv=c_kv,
                             conv_ch=conv_ch)
    return pl.pallas_call(
        body,
        grid=grid,
        in_specs=in_specs,
        out_specs=out_specs,
        out_shape=out_shapes,
        scratch_shapes=[pltpu.VMEM((nb, tt + HIST_ROWS, conv_ch), F32)],
        compiler_params=pltpu.CompilerParams(dimension_semantics=("arbitrary", "arbitrary"),
                                             vmem_limit_bytes=VMEM_LIMIT),
        name="stage_in_absorb" if absorb else "stage_in",
    )(x, hist, cs, *[wts[n] for n in names])


def _attn_prompt_body(q_ref, k_ref, vt_ref, o_ref, *, tq, qc):
    qi = pl.program_id(2)
    n_col = tq // qc
    chains = [(hh, cb) for hh in range(2) for cb in range(n_col)]
    slab = lambda hh: slice(hh * HEAD_SLAB, (hh + 1) * HEAD_SLAB)
    q_heads = [q_ref[0, :, slab(hh)] for hh in range(2)]

    def update(carry, s, vt, mask):
        m, l, acc = carry
        if mask is not None:
            s = jnp.where(mask, s, NEG_INF)
        m_new = jnp.maximum(m, jnp.max(s, axis=0, keepdims=True))
        alpha = jnp.exp2((m - m_new) * EXP2_SCALE)
        p = jnp.exp2((s - m_new) * EXP2_SCALE)
        l = alpha * l + jnp.sum(p, axis=0, keepdims=True)
        acc = alpha * acc + _dot(vt, p.astype(BF16))
        return m_new, l, acc

    def full_step(kt, carry):
        start = pl.multiple_of(kt * tq, tq)
        new = []
        for hh in range(2):
            k = k_ref[0, pl.ds(start, tq), slab(hh)]
            vt = vt_ref[0, hh * V_DIM:(hh + 1) * V_DIM, pl.ds(start, tq)]
            s = lax.dot_general(k, q_heads[hh], NT_DIMS, preferred_element_type=F32)
            for cb in range(n_col):
                new.append(update(carry[hh * n_col + cb], s[:, cb * qc:(cb + 1) * qc], vt, None))
        return tuple(new)

    init = tuple((jnp.full((1, qc), NEG_INF, F32), jnp.zeros((1, qc), F32), jnp.zeros((V_DIM, qc), F32))
                 for _ in chains)
    carry = lax.fori_loop(0, qi, full_step, init)

    start = pl.multiple_of(qi * tq, tq)
    for hh in range(2):
        for cb in range(n_col):
            n_keys = (cb + 1) * qc
            k = k_ref[0, pl.ds(start, n_keys), slab(hh)]
            vt = vt_ref[0, hh * V_DIM:(hh + 1) * V_DIM, pl.ds(start, n_keys)]
            s = lax.dot_general(k, q_heads[hh][cb * qc:(cb + 1) * qc], NT_DIMS, preferred_element_type=F32)
            key_chunk = lax.broadcasted_iota(jnp.int32, (n_keys, qc), 0) // CHUNK
            qry_chunk = (lax.broadcasted_iota(jnp.int32, (n_keys, qc), 1) + cb * qc) // CHUNK
            _, l, acc = update(carry[hh * n_col + cb], s, vt, key_chunk <= qry_chunk)
            o_ref[0, hh * V_DIM:(hh + 1) * V_DIM, cb * qc:(cb + 1) * qc] = (acc / l).astype(BF16)


def _attn_prompt(q, k, vt, *, tq, qc):
    b, t_len, _ = q.shape
    grid = (b, N_HEADS // 2, t_len // tq)
    return pl.pallas_call(
        functools.partial(_attn_prompt_body, tq=tq, qc=qc),
        grid=grid,
        in_specs=[pl.BlockSpec((1, tq, 2 * HEAD_SLAB), lambda i, p, j: (i, j, p)),
                  pl.BlockSpec((1, t_len, 2 * HEAD_SLAB), lambda i, p, j: (i, 0, p)),
                  pl.BlockSpec((1, 2 * V_DIM, t_len), lambda i, p, j: (i, p, 0))],
        out_specs=pl.BlockSpec((1, 2 * V_DIM, tq), lambda i, p, j: (i, p, j)),
        out_shape=jax.ShapeDtypeStruct((b, N_HEADS * V_DIM, t_len), BF16),
        compiler_params=pltpu.CompilerParams(dimension_semantics=("arbitrary",) * 3,
                                             vmem_limit_bytes=VMEM_LIMIT),
        name="attn_prompt",
    )(q, k, vt)


def _attn_sample_body(q_ref, qlat_ref, clat_ref, ckr_ref, nlat_ref, nkr_ref, o_ref, *, past, tt, c_kv):
    ql = jnp.concatenate([qlat_ref[0, :, hd * c_kv:(hd + 1) * c_kv] for hd in range(N_HEADS)], axis=0)
    qs = jnp.concatenate([q_ref[0, :, hd * HEAD_SLAB:(hd + 1) * HEAD_SLAB] for hd in range(N_HEADS)],
                         axis=0)
    qr = qs[:, QK_NOPE:QK_NOPE + QK_ROPE]
    nt = (((1,), (1,)), ((), ()))

    clat = clat_ref[0].astype(BF16)
    ckr = ckr_ref[0].astype(BF16)
    nlat = nlat_ref[0].astype(BF16)
    nkr = nkr_ref[0].astype(BF16)
    s_c = (lax.dot_general(ql, clat, nt, preferred_element_type=F32)
           + lax.dot_general(qr, ckr, nt, preferred_element_type=F32)) * SM_SCALE
    s_n = (lax.dot_general(ql, nlat, nt, preferred_element_type=F32)
           + lax.dot_general(qr, nkr, nt, preferred_element_type=F32)) * SM_SCALE
    rows = N_HEADS * tt
    q_pos = past + lax.broadcasted_iota(jnp.int32, (rows, tt), 0) % tt
    k_pos = past + lax.broadcasted_iota(jnp.int32, (rows, tt), 1)
    s_n = jnp.where(k_pos // CHUNK <= q_pos // CHUNK, s_n, NEG_INF)

    m = jnp.maximum(jnp.max(s_c, axis=-1, keepdims=True), jnp.max(s_n, axis=-1, keepdims=True))
    p_c = jnp.exp(s_c - m)
    p_n = jnp.exp(s_n - m)
    l = jnp.sum(p_c, axis=-1, keepdims=True) + jnp.sum(p_n, axis=-1, keepdims=True)
    o = (_dot(p_c.astype(BF16), clat) + _dot(p_n.astype(BF16), nlat)) / l
    for hd in range(N_HEADS):
        o_ref[0, :, hd * c_kv:(hd + 1) * c_kv] = o[hd * tt:(hd + 1) * tt, :].astype(BF16)


def _attn_sample(q, qlat, clat, ckr, nlat, nkr):
    b, tt, _ = q.shape
    past, c_kv = clat.shape[1], clat.shape[2]

    def spec(a):
        return pl.BlockSpec((1,) + a.shape[1:], lambda i: (i, 0, 0))

    return pl.pallas_call(
        functools.partial(_attn_sample_body, past=past, tt=tt, c_kv=c_kv),
        grid=(b,),
        in_specs=[spec(a) for a in (q, qlat, clat, ckr, nlat, nkr)],
        out_specs=pl.BlockSpec((1, tt, N_HEADS * c_kv), lambda i: (i, 0, 0)),
        out_shape=jax.ShapeDtypeStruct((b, tt, N_HEADS * c_kv), BF16),
        compiler_params=pltpu.CompilerParams(dimension_semantics=("arbitrary",),
                                             vmem_limit_bytes=VMEM_LIMIT),
        name="attn_sample",
    )(q, qlat, clat, ckr, nlat, nkr)


def _stage_out_body(x_ref, o_ref, c_ref, p_ref, gmix_ref, wg_ref, wao_ref, wco_ref, wout_ref, gffn_ref,
                    wup_ref, wdn_ref, gple_ref, wpg_ref, wpp_ref, gfin_ref, *rest, absorb, ff_chunk):
    if absorb:
        wuv_ref, y_ref = rest
    else:
        (y_ref,) = rest
    x = x_ref[...]
    d = x.shape[-1]
    h = _rms(x, gmix_ref[...]).astype(BF16)
    g = _dot(h, wg_ref[...])
    if absorb:
        o = _dot(o_ref[...], wuv_ref[...]).astype(BF16)
        a = _dot(o, wao_ref[...])
    else:
        a = lax.dot_general(o_ref[0], wao_ref[...], TN_DIMS, preferred_element_type=F32)
    cb = _dot(c_ref[...], wco_ref[...])
    m = jax.nn.sigmoid(g[:, :d]) * a + jax.nn.sigmoid(g[:, d:]) * cb
    x = x + _dot(m.astype(BF16), wout_ref[...])

    h = _rms(x, gffn_ref[...]).astype(BF16)
    d_ff = wup_ref.shape[-1]
    ff = jnp.zeros_like(x)
    for c in range(d_ff // ff_chunk):
        sl = slice(c * ff_chunk, (c + 1) * ff_chunk)
        up = jnp.maximum(_dot(h, wup_ref[:, sl]), 0.0)
        ff = ff + _dot((up * up).astype(BF16), wdn_ref[sl, :])
    x = x + ff

    h = _rms(x, gple_ref[...]).astype(BF16)
    pg = jax.nn.sigmoid(_dot(h, wpg_ref[...]))
    x = x + pg * _dot(p_ref[...].astype(BF16), wpp_ref[...])
    y_ref[...] = _rms(x, gfin_ref[...])


def _stage_out(x, o, cact, p, wts, *, tm, absorb):
    n, d = x.shape
    names = ["gmix", "wg", "wao", "wco", "wout", "gffn", "wup", "wdn", "gple", "wpg", "wpp", "gfin"]
    if absorb:
        names.append("wuv_bd")

    def row_spec(a):
        return pl.BlockSpec((tm, a.shape[-1]), lambda i: (i, 0))

    if absorb:
        o_spec = row_spec(o)
    else:
        tiles_per_seq = o.shape[2] // tm
        o_spec = pl.BlockSpec((1, o.shape[1], tm), lambda i: (i // tiles_per_seq, 0, i % tiles_per_seq))

    return pl.pallas_call(
        functools.partial(_stage_out_body, absorb=absorb, ff_chunk=1024),
        grid=(n // tm,),
        in_specs=[row_spec(x), o_spec, row_spec(cact), row_spec(p)] + [_const_spec(wts[k].shape) for k in names],
        out_specs=pl.BlockSpec((tm, d), lambda i: (i, 0)),
        out_shape=jax.ShapeDtypeStruct((n, d), F32),
        compiler_params=pltpu.CompilerParams(dimension_semantics=("arbitrary",),
                                             vmem_limit_bytes=VMEM_LIMIT),
        name="stage_out_absorb" if absorb else "stage_out",
    )(x, o, cact, p, *[wts[k] for k in names])


def _rot_half_cols(w):
    half = w.shape[-1] // 2
    return jnp.concatenate([-w[..., half:], w[..., :half]], axis=-1)


def _head_slabs(nope, rope):
    c, hds = (nope if nope is not None else rope).shape[:2]
    parts = [nope if nope is not None else jnp.zeros((c, hds, QK_NOPE), F32),
             rope if rope is not None else jnp.zeros((c, hds, QK_ROPE), F32),
             jnp.zeros((c, hds, HEAD_SLAB - QK_NOPE - QK_ROPE), F32)]
    return jnp.concatenate(parts, axis=-1).reshape(c, hds * HEAD_SLAB)


def _prep_weights(norm_mix_g, w_in, q_norm_g, w_uq, kv_norm_g, w_uk, w_uv, w_attn_out, conv_w, conv_b,
                  conv_ln_g, conv_ln_b, w_conv_out, w_out, norm_ffn_g, w_ff_up, w_ff_down, ple_norm_g,
                  w_ple_gate, w_ple_proj, final_norm_g):
    d = w_in.shape[0]
    c_q = q_norm_g.shape[-1]
    c_kv = kv_norm_g.shape[-1]
    conv_ch = conv_b.shape[-1]
    o_kv = c_q
    o_kr = o_kv + c_kv
    o_conv = o_kr + QK_ROPE
    o_gate = o_conv + 2 * conv_ch
    w_kr = w_in[:, o_kr:o_conv]
    zn = jnp.zeros((d, QK_NOPE), F32)
    zp = jnp.zeros((d, HEAD_SLAB - QK_NOPE - QK_ROPE), F32)
    wa = jnp.concatenate([w_in[:, :o_kr], zn, w_kr, zp, zn, _rot_half_cols(w_kr), zp,
                          w_in[:, o_conv:o_gate]], axis=1)
    q_nope, q_rope = w_uq[..., :QK_NOPE], w_uq[..., QK_NOPE:]
    row = lambda v: v.reshape(1, -1).astype(F32)
    wukt = jnp.transpose(w_uk, (1, 2, 0))
    wukt = jnp.concatenate([wukt, jnp.zeros((N_HEADS, HEAD_SLAB - QK_NOPE, c_kv), F32)], axis=1)
    eye = jnp.eye(N_HEADS, dtype=F32)
    wuv_bd = jnp.einsum("chd,hg->hcgd", w_uv, eye).reshape(N_HEADS * c_kv, N_HEADS * V_DIM)
    return {
        "gmix": row(norm_mix_g), "wa": wa.astype(BF16), "gq": row(q_norm_g), "gkv": row(kv_norm_g),
        "wq": _head_slabs(q_nope, q_rope).astype(BF16),
        "wqr": _head_slabs(None, _rot_half_cols(q_rope)).astype(BF16),
        "wkn": _head_slabs(w_uk, None).astype(BF16),
        "wv": w_uv.reshape(c_kv, N_HEADS * V_DIM).T.astype(BF16),
        "wukt": wukt.astype(BF16), "wuv_bd": wuv_bd.astype(BF16),
        "bconv": row(conv_b), "wconv": conv_w.astype(F32), "lng": row(conv_ln_g), "lnb": row(conv_ln_b),
        "wg": w_in[:, o_gate:].astype(BF16), "wao": w_attn_out.astype(BF16),
        "wco": w_conv_out.astype(BF16), "wout": w_out.astype(BF16), "gffn": row(norm_ffn_g),
        "wup": w_ff_up.astype(BF16), "wdn": w_ff_down.astype(BF16), "gple": row(ple_norm_g),
        "wpg": w_ple_gate.astype(BF16), "wpp": w_ple_proj.astype(BF16), "gfin": row(final_norm_g),
    }


def _rope_tables(pos):
    half = QK_ROPE // 2
    inv = ROPE_THETA ** (-jnp.arange(half, dtype=F32) / half)
    ang = pos.astype(F32)[:, None] * inv[None, :]
    cos, sin = jnp.cos(ang), jnp.sin(ang)
    n = pos.shape[0]
    pad = jnp.zeros((n, HEAD_SLAB - QK_NOPE - QK_ROPE), F32)
    return jnp.concatenate([jnp.ones((n, QK_NOPE), F32), cos, cos, pad,
                            jnp.zeros((n, QK_NOPE), F32), sin, sin, pad], axis=1)


def _tile(n, pref):
    return pref if n % pref == 0 else n


def kernel(x_prompt, x_sample, p_prompt, p_sample, cache_kv_latent, cache_k_rope, cache_conv, norm_mix_g, w_in, q_norm_g, w_uq, kv_norm_g, w_uk, w_uv, w_attn_out, conv_w, conv_b, conv_ln_g, conv_ln_b, w_conv_out, w_out, norm_ffn_g, w_ff_up, w_ff_down, ple_norm_g, w_ple_gate, w_ple_proj, final_norm_g):
    depth = w_in.shape[0]
    assert depth == 1, "one layer: the two request groups are independent within it"
    bp, tp, d = x_prompt.shape
    bs, ts, _ = x_sample.shape
    past = cache_kv_latent.shape[2]
    conv_ch = conv_b.shape[-1]
    wts = _prep_weights(norm_mix_g[0], w_in[0], q_norm_g[0], w_uq[0], kv_norm_g[0], w_uk[0], w_uv[0],
                        w_attn_out[0], conv_w[0], conv_b[0], conv_ln_g[0], conv_ln_b[0], w_conv_out[0],
                        w_out[0], norm_ffn_g[0], w_ff_up[0], w_ff_down[0], ple_norm_g[0], w_ple_gate[0],
                        w_ple_proj[0], final_norm_g)

    tt = _tile(tp, 512)
    hist0 = jnp.zeros((bp, HIST_ROWS, conv_ch), F32)
    q, k, vt, lat_p, kr_p, cact_p, ncv_p = _stage_in(x_prompt, hist0, _rope_tables(jnp.arange(tp)), wts,
                                                      nb=1, tt=tt, absorb=False)
    o_p = _attn_prompt(q, k, vt, tq=ATTN_TQ, qc=ATTN_QC)
    n_p = bp * tp
    y_p = _stage_out(x_prompt.reshape(n_p, d), o_p, cact_p.reshape(n_p, -1),
                     p_prompt[0].reshape(n_p, -1), wts, tm=tt, absorb=False)

    hist_s = jnp.pad(cache_conv[0], ((0, 0), (HIST_PAD, 0), (0, 0)))
    qs, qlat, lat_s, kr_s, cact_s, ncv_s = _stage_in(x_sample, hist_s,
                                                     _rope_tables(past + jnp.arange(ts)), wts,
                                                     nb=bs, tt=ts, absorb=True)
    o_s = _attn_sample(qs, qlat, cache_kv_latent[0], cache_k_rope[0], lat_s, kr_s)
    n_s = bs * ts
    y_s = _stage_out(x_sample.reshape(n_s, d), o_s.reshape(n_s, -1), cact_s.reshape(n_s, -1),
                     p_sample[0].reshape(n_s, -1), wts, tm=_tile(n_s, 512), absorb=True)

    return (y_p.reshape(bp, tp, d), y_s.reshape(bs, ts, d), lat_p[None], kr_p[None], ncv_p[None],
            lat_s[None], kr_s[None], ncv_s[None])
```

```python
import functools
import math

import jax
import jax.numpy as jnp
from jax import lax
from jax.experimental import pallas as pl
from jax.experimental.pallas import tpu as pltpu

F32 = jnp.float32
BF16 = jnp.bfloat16

CHUNK = 64
N_HEADS = 8
QK_NOPE = 64
QK_ROPE = 32
V_DIM = 64
CONV_W = 31
ROPE_THETA = 10000.0
RMS_EPS = 1e-6
LN_EPS = 1e-5
SM_SCALE = (QK_NOPE + QK_ROPE) ** -0.5
EXP2_SCALE = SM_SCALE * math.log2(math.e)
NEG_INF = -1e30

LANES = 128
SUBLANES = 8
HEAD_SLAB = LANES
HIST_ROWS = 32
HIST_PAD = HIST_ROWS - (CONV_W - 1)
ATTN_TQ = 512
ATTN_QC = 256
BF16_SUBLANES = 16
ACC_ROWS = V_DIM + BF16_SUBLANES
VMEM_LIMIT = 56 * 1024 * 1024


def _rms(x, g):
    return x * lax.rsqrt(jnp.mean(x * x, axis=-1, keepdims=True) + RMS_EPS) * g


def _dot(a, b):
    return jnp.dot(a, b, preferred_element_type=F32)


NT_DIMS = (((1,), (1,)), ((), ()))
TN_DIMS = (((0,), (0,)), ((), ()))


def _const_spec(shape):
    nd = len(shape)
    return pl.BlockSpec(shape, lambda *_: (0,) * nd, pipeline_mode=pl.Buffered(1))


def _stage_in_body(x_ref, hist_ref, cs_ref, gmix_ref, wa_ref, gq_ref, gkv_ref, wq_ref, wqr_ref,
                   *rest, nb, tt, absorb, c_q, c_kv, conv_ch):
    if absorb:
        wukt_ref, bconv_ref, wconv_ref, lng_ref, lnb_ref = rest[:5]
        outs = rest[5:]
        q_ref, qlat_ref, lat_ref, kr_ref, cact_ref, ncv_ref, uext_ref, shift_ref = outs
    else:
        wkn_ref, wv_ref, bconv_ref, wconv_ref, lng_ref, lnb_ref = rest[:6]
        outs = rest[6:]
        q_ref, k_ref, v_ref, lat_ref, kr_ref, cact_ref, ncv_ref, uext_ref, shift_ref = outs

    t = pl.program_id(1)
    rows = nb * tt
    d = x_ref.shape[-1]

    x = x_ref[...].reshape(rows, d)
    h = _rms(x, gmix_ref[...]).astype(BF16)
    z = _dot(h, wa_ref[...])
    o_kv = c_q
    o_ks = o_kv + c_kv
    o_kr = o_ks + HEAD_SLAB
    o_u = o_kr + HEAD_SLAB

    cos_q, sin_q, cos_k, sin_k = (cs_ref[:, i * HEAD_SLAB:(i + 1) * HEAD_SLAB] for i in range(4))

    def rope3(a, a_rot, cos, sin):
        a3 = a.reshape(nb, tt, HEAD_SLAB) * cos[None] + a_rot.reshape(nb, tt, HEAD_SLAB) * sin[None]
        return a3.reshape(rows, HEAD_SLAB)

    cqn = _rms(z[:, :c_q], gq_ref[...]).astype(BF16)
    qa = _dot(cqn, wq_ref[...])
    qb = _dot(cqn, wqr_ref[...])
    q_heads = []
    for hd in range(N_HEADS):
        sl = slice(hd * HEAD_SLAB, (hd + 1) * HEAD_SLAB)
        qh = rope3(qa[:, sl], qb[:, sl], cos_q, sin_q).astype(BF16)
        q_heads.append(qh)
        q_ref[:, :, sl] = qh.reshape(nb, tt, HEAD_SLAB)

    latent = _rms(z[:, o_kv:o_ks], gkv_ref[...])
    lat_ref[...] = latent.reshape(nb, tt, c_kv)
    kslab = rope3(z[:, o_ks:o_kr], z[:, o_kr:o_u], cos_k, sin_k)
    kr_ref[...] = kslab[:, QK_NOPE:QK_NOPE + QK_ROPE].reshape(nb, tt, QK_ROPE)

    if absorb:
        for hd in range(N_HEADS):
            ql = _dot(q_heads[hd], wukt_ref[hd])
            qlat_ref[:, :, hd * c_kv:(hd + 1) * c_kv] = ql.astype(BF16).reshape(nb, tt, c_kv)
    else:
        lat_b = latent.astype(BF16)
        kn = _dot(lat_b, wkn_ref[...])
        for hd in range(N_HEADS):
            sl = slice(hd * HEAD_SLAB, (hd + 1) * HEAD_SLAB)
            k_ref[:, :, sl] = (kn[:, sl] + kslab).astype(BF16).reshape(nb, tt, HEAD_SLAB)
        v_ref[0] = lax.dot_general(wv_ref[...], lat_b, NT_DIMS, preferred_element_type=F32).astype(BF16)

    u = z[:, o_u:o_u + conv_ch] * jax.nn.sigmoid(z[:, o_u + conv_ch:o_u + 2 * conv_ch])

    @pl.when(t == 0)
    def _():
        uext_ref[:, :HIST_ROWS, :] = hist_ref[...]

    uext_ref[:, HIST_ROWS:, :] = u.reshape(nb, tt, conv_ch)
    dw = jnp.zeros((nb, tt, conv_ch), F32) + bconv_ref[...][None]
    for r in range(SUBLANES):
        offs = [o for o in range(HIST_PAD, HIST_PAD + CONV_W) if o % SUBLANES == r]
        src_ref = uext_ref
        if r:
            n_rows = tt + max(offs) - r
            shift_ref[:, :n_rows, :] = uext_ref[:, r:r + n_rows, :]
            src_ref = shift_ref
        for o in offs:
            k = o - HIST_PAD
            dw = dw + src_ref[:, o - r:o - r + tt, :] * wconv_ref[k:k + 1, :][None]
    mu = jnp.mean(dw, axis=-1, keepdims=True)
    dc = dw - mu
    ln = dc * lax.rsqrt(jnp.mean(dc * dc, axis=-1, keepdims=True) + LN_EPS)
    ln = ln * lng_ref[...][None] + lnb_ref[...][None]
    cact_ref[...] = (ln * jax.nn.sigmoid(ln)).astype(BF16)

    @pl.when(t == pl.num_programs(1) - 1)
    def _():
        ncv_ref[...] = uext_ref[:, tt + HIST_PAD:tt + HIST_ROWS, :]

    uext_ref[:, :HIST_ROWS, :] = uext_ref[:, tt:tt + HIST_ROWS, :]


def _stage_in(x, hist, cs, wts, *, nb, tt, absorb):
    b, t_len, d = x.shape
    c_q = wts["gq"].shape[-1]
    c_kv = wts["gkv"].shape[-1]
    conv_ch = wts["bconv"].shape[-1]
    grid = (b // nb, t_len // tt)

    def row_spec(width):
        return pl.BlockSpec((nb, tt, width), lambda i, j: (i, j, 0))

    in_specs = [
        row_spec(d),
        pl.BlockSpec((nb, HIST_ROWS, conv_ch), lambda i, j: (i, 0, 0)),
        pl.BlockSpec((tt, 4 * HEAD_SLAB), lambda i, j: (j, 0)),
    ]
    names = ["gmix", "wa", "gq", "gkv", "wq", "wqr"]
    names += ["wukt"] if absorb else ["wkn", "wv"]
    names += ["bconv", "wconv", "lng", "lnb"]
    in_specs += [_const_spec(wts[n].shape) for n in names]

    qk_w = N_HEADS * HEAD_SLAB
    out_shapes = [jax.ShapeDtypeStruct((b, t_len, qk_w), BF16)]
    out_specs = [row_spec(qk_w)]
    if absorb:
        out_shapes.append(jax.ShapeDtypeStruct((b, t_len, N_HEADS * c_kv), BF16))
        out_specs.append(row_spec(N_HEADS * c_kv))
    else:
        assert nb == 1, "transposed value tiles are written one sequence at a time"
        out_shapes += [jax.ShapeDtypeStruct((b, t_len, qk_w), BF16),
                       jax.ShapeDtypeStruct((b, N_HEADS * V_DIM, t_len), BF16)]
        out_specs += [row_spec(qk_w), pl.BlockSpec((1, N_HEADS * V_DIM, tt), lambda i, j: (i, 0, j))]
    out_shapes += [jax.ShapeDtypeStruct((b, t_len, c_kv), F32),
                   jax.ShapeDtypeStruct((b, t_len, QK_ROPE), F32),
                   jax.ShapeDtypeStruct((b, t_len, conv_ch), BF16),
                   jax.ShapeDtypeStruct((b, CONV_W - 1, conv_ch), F32)]
    out_specs += [row_spec(c_kv), row_spec(QK_ROPE), row_spec(conv_ch),
                  pl.BlockSpec((nb, CONV_W - 1, conv_ch), lambda i, j: (i, 0, 0))]

    body = functools.partial(_stage_in_body, nb=nb, tt=tt, absorb=absorb, c_q=c_q, c_kv=c_kv,
                             conv_ch=conv_ch)
    return pl.pallas_call(
        body,
        grid=grid,
        in_specs=in_specs,
        out_specs=out_specs,
        out_shape=out_shapes,
        scratch_shapes=[pltpu.VMEM((nb, tt + HIST_ROWS, conv_ch), F32),
                        pltpu.VMEM((nb, tt + HIST_ROWS, conv_ch), F32)],
        compiler_params=pltpu.CompilerParams(dimension_semantics=("arbitrary", "arbitrary"),
                                             vmem_limit_bytes=VMEM_LIMIT),
        name="stage_in_absorb" if absorb else "stage_in",
    )(x, hist, cs, *[wts[n] for n in names])


def _attn_prompt_body(q_ref, k_ref, vt_ref, o_ref, sa_ref, sb_ref, m_ref, acc_ref, *, tq, qc):
    qi = pl.program_id(2)
    n_col = tq // qc
    slab = lambda hh: slice(hh * HEAD_SLAB, (hh + 1) * HEAD_SLAB)
    ones_rows = jnp.ones((ACC_ROWS - V_DIM, tq), BF16)

    def scores_into(kt, s_buf):
        start = pl.multiple_of(kt * tq, tq)
        for hh in range(2):
            k = k_ref[0, pl.ds(start, tq), slab(hh)]
            s_buf[hh] = lax.dot_general(k, q_ref[0, :, slab(hh)], NT_DIMS, preferred_element_type=F32)

    def update(c, s, vt, mask):
        if mask is not None:
            s = jnp.where(mask, s, NEG_INF)
        m = m_ref[c]
        m_new = jnp.maximum(m, jnp.max(s, axis=0, keepdims=True))
        p = jnp.exp2(s - m_new).astype(BF16)
        vt_aug = jnp.concatenate([vt, ones_rows[:, :vt.shape[1]]], axis=0)
        acc_ref[c] = jnp.exp2(m - m_new) * acc_ref[c] + _dot(vt_aug, p)
        m_ref[c] = m_new

    def full_step(kt, s_cur, s_nxt):
        scores_into(kt + 1, s_nxt)
        start = pl.multiple_of(kt * tq, tq)
        for hh in range(2):
            vt = vt_ref[0, hh * V_DIM:(hh + 1) * V_DIM, pl.ds(start, tq)]
            for cb in range(n_col):
                update(hh * n_col + cb, s_cur[hh, :, cb * qc:(cb + 1) * qc], vt, None)

    def diag_step(s_cur):
        start = pl.multiple_of(qi * tq, tq)
        for hh in range(2):
            for cb in range(n_col):
                c = hh * n_col + cb
                n_keys = (cb + 1) * qc
                vt = vt_ref[0, hh * V_DIM:(hh + 1) * V_DIM, pl.ds(start, n_keys)]
                key_chunk = lax.broadcasted_iota(jnp.int32, (n_keys, qc), 0) // CHUNK
                qry_chunk = (lax.broadcasted_iota(jnp.int32, (n_keys, qc), 1) + cb * qc) // CHUNK
                update(c, s_cur[hh, :n_keys, cb * qc:(cb + 1) * qc], vt, key_chunk <= qry_chunk)
                acc = acc_ref[c]
                o_ref[0, hh * V_DIM:(hh + 1) * V_DIM, cb * qc:(cb + 1) * qc] = (
                    acc[:V_DIM] / acc[V_DIM:V_DIM + 1]).astype(BF16)

    m_ref[...] = jnp.full(m_ref.shape, NEG_INF, F32)
    acc_ref[...] = jnp.zeros(acc_ref.shape, F32)
    scores_into(0, sa_ref)

    def pair_body(j, carry):
        full_step(2 * j, sa_ref, sb_ref)
        full_step(2 * j + 1, sb_ref, sa_ref)
        return carry

    lax.fori_loop(0, qi // 2, pair_body, 0)

    @pl.when(qi % 2 == 1)
    def _():
        full_step(qi - 1, sa_ref, sb_ref)
        diag_step(sb_ref)

    @pl.when(qi % 2 == 0)
    def _():
        diag_step(sa_ref)


def _attn_prompt(q, k, vt, *, tq, qc):
    b, t_len, _ = q.shape
    grid = (b, N_HEADS // 2, t_len // tq)
    n_chains = 2 * (tq // qc)
    return pl.pallas_call(
        functools.partial(_attn_prompt_body, tq=tq, qc=qc),
        grid=grid,
        in_specs=[pl.BlockSpec((1, tq, 2 * HEAD_SLAB), lambda i, p, j: (i, j, p)),
                  pl.BlockSpec((1, t_len, 2 * HEAD_SLAB), lambda i, p, j: (i, 0, p)),
                  pl.BlockSpec((1, 2 * V_DIM, t_len), lambda i, p, j: (i, p, 0))],
        out_specs=pl.BlockSpec((1, 2 * V_DIM, tq), lambda i, p, j: (i, p, j)),
        out_shape=jax.ShapeDtypeStruct((b, N_HEADS * V_DIM, t_len), BF16),
        scratch_shapes=[pltpu.VMEM((2, tq, tq), F32), pltpu.VMEM((2, tq, tq), F32),
                        pltpu.VMEM((n_chains, 1, qc), F32), pltpu.VMEM((n_chains, ACC_ROWS, qc), F32)],
        compiler_params=pltpu.CompilerParams(dimension_semantics=("arbitrary",) * 3,
                                             vmem_limit_bytes=VMEM_LIMIT),
        name="attn_prompt",
    )(q, k, vt)


def _attn_sample_body(q_ref, qlat_ref, clat_ref, ckr_ref, nlat_ref, nkr_ref, o_ref, *, past, tt, c_kv):
    ql = jnp.concatenate([qlat_ref[0, :, hd * c_kv:(hd + 1) * c_kv] for hd in range(N_HEADS)], axis=0)
    qs = jnp.concatenate([q_ref[0, :, hd * HEAD_SLAB:(hd + 1) * HEAD_SLAB] for hd in range(N_HEADS)],
                         axis=0)
    qr = qs[:, QK_NOPE:QK_NOPE + QK_ROPE]
    nt = (((1,), (1,)), ((), ()))

    clat = clat_ref[0].astype(BF16)
    ckr = ckr_ref[0].astype(BF16)
    nlat = nlat_ref[0].astype(BF16)
    nkr = nkr_ref[0].astype(BF16)
    s_c = (lax.dot_general(ql, clat, nt, preferred_element_type=F32)
           + lax.dot_general(qr, ckr, nt, preferred_element_type=F32)) * SM_SCALE
    s_n = (lax.dot_general(ql, nlat, nt, preferred_element_type=F32)
           + lax.dot_general(qr, nkr, nt, preferred_element_type=F32)) * SM_SCALE
    rows = N_HEADS * tt
    q_pos = past + lax.broadcasted_iota(jnp.int32, (rows, tt), 0) % tt
    k_pos = past + lax.broadcasted_iota(jnp.int32, (rows, tt), 1)
    s_n = jnp.where(k_pos // CHUNK <= q_pos // CHUNK, s_n, NEG_INF)

    m = jnp.maximum(jnp.max(s_c, axis=-1, keepdims=True), jnp.max(s_n, axis=-1, keepdims=True))
    p_c = jnp.exp(s_c - m)
    p_n = jnp.exp(s_n - m)
    l = jnp.sum(p_c, axis=-1, keepdims=True) + jnp.sum(p_n, axis=-1, keepdims=True)
    o = (_dot(p_c.astype(BF16), clat) + _dot(p_n.astype(BF16), nlat)) / l
    for hd in range(N_HEADS):
        o_ref[0, :, hd * c_kv:(hd + 1) * c_kv] = o[hd * tt:(hd + 1) * tt, :].astype(BF16)


def _attn_sample(q, qlat, clat, ckr, nlat, nkr):
    b, tt, _ = q.shape
    past, c_kv = clat.shape[1], clat.shape[2]

    def spec(a):
        return pl.BlockSpec((1,) + a.shape[1:], lambda i: (i, 0, 0))

    return pl.pallas_call(
        functools.partial(_attn_sample_body, past=past, tt=tt, c_kv=c_kv),
        grid=(b,),
        in_specs=[spec(a) for a in (q, qlat, clat, ckr, nlat, nkr)],
        out_specs=pl.BlockSpec((1, tt, N_HEADS * c_kv), lambda i: (i, 0, 0)),
        out_shape=jax.ShapeDtypeStruct((b, tt, N_HEADS * c_kv), BF16),
        compiler_params=pltpu.CompilerParams(dimension_semantics=("arbitrary",),
                                             vmem_limit_bytes=VMEM_LIMIT),
        name="attn_sample",
    )(q, qlat, clat, ckr, nlat, nkr)


def _stage_out_body(x_ref, o_ref, c_ref, p_ref, gmix_ref, wg_ref, wao_ref, wco_ref, wout_ref, gffn_ref,
                    wup_ref, wdn_ref, gple_ref, wpg_ref, wpp_ref, gfin_ref, *rest, absorb, ff_chunk):
    if absorb:
        wuv_ref, y_ref = rest
    else:
        (y_ref,) = rest
    x = x_ref[...]
    d = x.shape[-1]
    h = _rms(x, gmix_ref[...]).astype(BF16)
    g = _dot(h, wg_ref[...])
    if absorb:
        o = _dot(o_ref[...], wuv_ref[...]).astype(BF16)
        a = _dot(o, wao_ref[...])
    else:
        a = lax.dot_general(o_ref[0], wao_ref[...], TN_DIMS, preferred_element_type=F32)
    cb = _dot(c_ref[...], wco_ref[...])
    m = jax.nn.sigmoid(g[:, :d]) * a + jax.nn.sigmoid(g[:, d:]) * cb
    x = x + _dot(m.astype(BF16), wout_ref[...])

    h = _rms(x, gffn_ref[...]).astype(BF16)
    d_ff = wup_ref.shape[-1]
    ff = jnp.zeros_like(x)
    for c in range(d_ff // ff_chunk):
        sl = slice(c * ff_chunk, (c + 1) * ff_chunk)
        up = jnp.maximum(_dot(h, wup_ref[:, sl]), 0.0)
        ff = ff + _dot((up * up).astype(BF16), wdn_ref[sl, :])
    x = x + ff

    h = _rms(x, gple_ref[...]).astype(BF16)
    pg = jax.nn.sigmoid(_dot(h, wpg_ref[...]))
    x = x + pg * _dot(p_ref[...].astype(BF16), wpp_ref[...])
    y_ref[...] = _rms(x, gfin_ref[...])


def _stage_out(x, o, cact, p, wts, *, tm, absorb):
    n, d = x.shape
    names = ["gmix", "wg", "wao", "wco", "wout", "gffn", "wup", "wdn", "gple", "wpg", "wpp", "gfin"]
    if absorb:
        names.append("wuv_bd")

    def row_spec(a):
        return pl.BlockSpec((tm, a.shape[-1]), lambda i: (i, 0))

    if absorb:
        o_spec = row_spec(o)
    else:
        tiles_per_seq = o.shape[2] // tm
        o_spec = pl.BlockSpec((1, o.shape[1], tm), lambda i: (i // tiles_per_seq, 0, i % tiles_per_seq))

    return pl.pallas_call(
        functools.partial(_stage_out_body, absorb=absorb, ff_chunk=1024),
        grid=(n // tm,),
        in_specs=[row_spec(x), o_spec, row_spec(cact), row_spec(p)] + [_const_spec(wts[k].shape) for k in names],
        out_specs=pl.BlockSpec((tm, d), lambda i: (i, 0)),
        out_shape=jax.ShapeDtypeStruct((n, d), F32),
        compiler_params=pltpu.CompilerParams(dimension_semantics=("arbitrary",),
                                             vmem_limit_bytes=VMEM_LIMIT),
        name="stage_out_absorb" if absorb else "stage_out",
    )(x, o, cact, p, *[wts[k] for k in names])


def _rot_half_cols(w):
    half = w.shape[-1] // 2
    return jnp.concatenate([-w[..., half:], w[..., :half]], axis=-1)


def _head_slabs(nope, rope):
    c, hds = (nope if nope is not None else rope).shape[:2]
    parts = [nope if nope is not None else jnp.zeros((c, hds, QK_NOPE), F32),
             rope if rope is not None else jnp.zeros((c, hds, QK_ROPE), F32),
             jnp.zeros((c, hds, HEAD_SLAB - QK_NOPE - QK_ROPE), F32)]
    return jnp.concatenate(parts, axis=-1).reshape(c, hds * HEAD_SLAB)


def _prep_weights(norm_mix_g, w_in, q_norm_g, w_uq, kv_norm_g, w_uk, w_uv, w_attn_out, conv_w, conv_b,
                  conv_ln_g, conv_ln_b, w_conv_out, w_out, norm_ffn_g, w_ff_up, w_ff_down, ple_norm_g,
                  w_ple_gate, w_ple_proj, final_norm_g):
    d = w_in.shape[0]
    c_q = q_norm_g.shape[-1]
    c_kv = kv_norm_g.shape[-1]
    conv_ch = conv_b.shape[-1]
    o_kv = c_q
    o_kr = o_kv + c_kv
    o_conv = o_kr + QK_ROPE
    o_gate = o_conv + 2 * conv_ch
    w_kr = w_in[:, o_kr:o_conv]
    zn = jnp.zeros((d, QK_NOPE), F32)
    zp = jnp.zeros((d, HEAD_SLAB - QK_NOPE - QK_ROPE), F32)
    wa = jnp.concatenate([w_in[:, :o_kr], zn, w_kr, zp, zn, _rot_half_cols(w_kr), zp,
                          w_in[:, o_conv:o_gate]], axis=1)
    q_nope, q_rope = w_uq[..., :QK_NOPE], w_uq[..., QK_NOPE:]
    row = lambda v: v.reshape(1, -1).astype(F32)
    wukt = jnp.transpose(w_uk, (1, 2, 0))
    wukt = jnp.concatenate([wukt, jnp.zeros((N_HEADS, HEAD_SLAB - QK_NOPE, c_kv), F32)], axis=1)
    eye = jnp.eye(N_HEADS, dtype=F32)
    wuv_bd = jnp.einsum("chd,hg->hcgd", w_uv, eye).reshape(N_HEADS * c_kv, N_HEADS * V_DIM)
    return {
        "gmix": row(norm_mix_g), "wa": wa.astype(BF16), "gq": row(q_norm_g), "gkv": row(kv_norm_g),
        "wq": _head_slabs(q_nope, q_rope).astype(BF16),
        "wqr": _head_slabs(None, _rot_half_cols(q_rope)).astype(BF16),
        "wkn": _head_slabs(w_uk, None).astype(BF16),
        "wv": w_uv.reshape(c_kv, N_HEADS * V_DIM).T.astype(BF16),
        "wukt": wukt.astype(BF16), "wuv_bd": wuv_bd.astype(BF16),
        "bconv": row(conv_b), "wconv": conv_w.astype(F32), "lng": row(conv_ln_g), "lnb": row(conv_ln_b),
        "wg": w_in[:, o_gate:].astype(BF16), "wao": w_attn_out.astype(BF16),
        "wco": w_conv_out.astype(BF16), "wout": w_out.astype(BF16), "gffn": row(norm_ffn_g),
        "wup": w_ff_up.astype(BF16), "wdn": w_ff_down.astype(BF16), "gple": row(ple_norm_g),
        "wpg": w_ple_gate.astype(BF16), "wpp": w_ple_proj.astype(BF16), "gfin": row(final_norm_g),
    }


def _rope_tables(pos, q_scale):
    half = QK_ROPE // 2
    inv = ROPE_THETA ** (-jnp.arange(half, dtype=F32) / half)
    ang = pos.astype(F32)[:, None] * inv[None, :]
    cos, sin = jnp.cos(ang), jnp.sin(ang)
    n = pos.shape[0]
    pad = jnp.zeros((n, HEAD_SLAB - QK_NOPE - QK_ROPE), F32)
    cos_t = jnp.concatenate([jnp.ones((n, QK_NOPE), F32), cos, cos, pad], axis=1)
    sin_t = jnp.concatenate([jnp.zeros((n, QK_NOPE), F32), sin, sin, pad], axis=1)
    return jnp.concatenate([cos_t * q_scale, sin_t * q_scale, cos_t, sin_t], axis=1)


def _tile(n, pref):
    return pref if n % pref == 0 else n


def kernel(x_prompt, x_sample, p_prompt, p_sample, cache_kv_latent, cache_k_rope, cache_conv, norm_mix_g, w_in, q_norm_g, w_uq, kv_norm_g, w_uk, w_uv, w_attn_out, conv_w, conv_b, conv_ln_g, conv_ln_b, w_conv_out, w_out, norm_ffn_g, w_ff_up, w_ff_down, ple_norm_g, w_ple_gate, w_ple_proj, final_norm_g):
    depth = w_in.shape[0]
    assert depth == 1, "one layer: the two request groups are independent within it"
    bp, tp, d = x_prompt.shape
    bs, ts, _ = x_sample.shape
    past = cache_kv_latent.shape[2]
    conv_ch = conv_b.shape[-1]
    wts = _prep_weights(norm_mix_g[0], w_in[0], q_norm_g[0], w_uq[0], kv_norm_g[0], w_uk[0], w_uv[0],
                        w_attn_out[0], conv_w[0], conv_b[0], conv_ln_g[0], conv_ln_b[0], w_conv_out[0],
                        w_out[0], norm_ffn_g[0], w_ff_up[0], w_ff_down[0], ple_norm_g[0], w_ple_gate[0],
                        w_ple_proj[0], final_norm_g)

    tt = _tile(tp, 512)
    hist0 = jnp.zeros((bp, HIST_ROWS, conv_ch), F32)
    q, k, vt, lat_p, kr_p, cact_p, ncv_p = _stage_in(x_prompt, hist0, _rope_tables(jnp.arange(tp), EXP2_SCALE), wts,
                                                      nb=1, tt=tt, absorb=False)
    o_p = _attn_prompt(q, k, vt, tq=ATTN_TQ, qc=ATTN_QC)
    n_p = bp * tp
    y_p = _stage_out(x_prompt.reshape(n_p, d), o_p, cact_p.reshape(n_p, -1),
                     p_prompt[0].reshape(n_p, -1), wts, tm=tt, absorb=False)

    hist_s = jnp.pad(cache_conv[0], ((0, 0), (HIST_PAD, 0), (0, 0)))
    qs, qlat, lat_s, kr_s, cact_s, ncv_s = _stage_in(x_sample, hist_s,
                                                     _rope_tables(past + jnp.arange(ts), 1.0), wts,
                                                     nb=bs, tt=ts, absorb=True)
    o_s = _attn_sample(qs, qlat, cache_kv_latent[0], cache_k_rope[0], lat_s, kr_s)
    n_s = bs * ts
    y_s = _stage_out(x_sample.reshape(n_s, d), o_s.reshape(n_s, -1), cact_s.reshape(n_s, -1),
                     p_sample[0].reshape(n_s, -1), wts, tm=_tile(n_s, 512), absorb=True)

    return (y_p.reshape(bp, tp, d), y_s.reshape(bs, ts, d), lat_p[None], kr_p[None], ncv_p[None],
            lat_s[None], kr_s[None], ncv_s[None])
```

```python
import functools
import math

import jax
import jax.numpy as jnp
from jax import lax
from jax.experimental import pallas as pl
from jax.experimental.pallas import tpu as pltpu

F32 = jnp.float32
BF16 = jnp.bfloat16

CHUNK = 64
N_HEADS = 8
QK_NOPE = 64
QK_ROPE = 32
V_DIM = 64
CONV_W = 31
ROPE_THETA = 10000.0
RMS_EPS = 1e-6
LN_EPS = 1e-5
SM_SCALE = (QK_NOPE + QK_ROPE) ** -0.5
EXP2_SCALE = SM_SCALE * math.log2(math.e)
NEG_INF = -1e30

LANES = 128
SUBLANES = 8
HEAD_SLAB = LANES
HIST_ROWS = 32
HIST_PAD = HIST_ROWS - (CONV_W - 1)
ATTN_TQ = 1024
ATTN_QC = 256
BF16_SUBLANES = 16
ACC_ROWS = V_DIM + BF16_SUBLANES
VMEM_LIMIT = 56 * 1024 * 1024


def _rms(x, g):
    return x * lax.rsqrt(jnp.mean(x * x, axis=-1, keepdims=True) + RMS_EPS) * g


def _dot(a, b):
    return jnp.dot(a, b, preferred_element_type=F32)


NT_DIMS = (((1,), (1,)), ((), ()))
TN_DIMS = (((0,), (0,)), ((), ()))


def _const_spec(shape):
    nd = len(shape)
    return pl.BlockSpec(shape, lambda *_: (0,) * nd, pipeline_mode=pl.Buffered(1))


def _stage_in_body(x_ref, cs_ref, gmix_ref, wa_ref, wg_ref, gq_ref, gkv_ref, wq_ref, wqr_ref,
                   *rest, nb, tt, absorb, c_q, c_kv, conv_ch):
    if absorb:
        (wukt_ref,) = rest[:1]
        q_ref, qlat_ref, lat_ref, kr_ref, u_ref, sg_ref = rest[1:]
    else:
        wkn_ref, wv_ref = rest[:2]
        q_ref, k_ref, v_ref, lat_ref, kr_ref, u_ref, sg_ref = rest[2:]

    rows = nb * tt
    d = x_ref.shape[-1]

    x = x_ref[...].reshape(rows, d)
    h = _rms(x, gmix_ref[...]).astype(BF16)
    z = _dot(h, wa_ref[...])
    o_kv = c_q
    o_ks = o_kv + c_kv
    o_kr = o_ks + HEAD_SLAB
    o_u = o_kr + HEAD_SLAB

    sg_ref[...] = jax.nn.sigmoid(_dot(h, wg_ref[...])).astype(BF16).reshape(nb, tt, sg_ref.shape[-1])

    cos_q, sin_q, cos_k, sin_k = (cs_ref[:, i * HEAD_SLAB:(i + 1) * HEAD_SLAB] for i in range(4))

    def rope3(a, a_rot, cos, sin):
        a3 = a.reshape(nb, tt, HEAD_SLAB) * cos[None] + a_rot.reshape(nb, tt, HEAD_SLAB) * sin[None]
        return a3.reshape(rows, HEAD_SLAB)

    cqn = _rms(z[:, :c_q], gq_ref[...]).astype(BF16)
    qa = _dot(cqn, wq_ref[...])
    qb = _dot(cqn, wqr_ref[...])
    q_heads = []
    for hd in range(N_HEADS):
        sl = slice(hd * HEAD_SLAB, (hd + 1) * HEAD_SLAB)
        qh = rope3(qa[:, sl], qb[:, sl], cos_q, sin_q).astype(BF16)
        q_heads.append(qh)
        q_ref[:, :, sl] = qh.reshape(nb, tt, HEAD_SLAB)

    latent = _rms(z[:, o_kv:o_ks], gkv_ref[...])
    lat_ref[...] = latent.reshape(nb, tt, c_kv)
    kslab = rope3(z[:, o_ks:o_kr], z[:, o_kr:o_u], cos_k, sin_k)
    kr_ref[...] = kslab[:, QK_NOPE:QK_NOPE + QK_ROPE].reshape(nb, tt, QK_ROPE)

    if absorb:
        for hd in range(N_HEADS):
            ql = _dot(q_heads[hd], wukt_ref[hd])
            qlat_ref[:, :, hd * c_kv:(hd + 1) * c_kv] = ql.astype(BF16).reshape(nb, tt, c_kv)
    else:
        lat_b = latent.astype(BF16)
        kn = _dot(lat_b, wkn_ref[...])
        for hd in range(N_HEADS):
            sl = slice(hd * HEAD_SLAB, (hd + 1) * HEAD_SLAB)
            k_ref[:, :, sl] = (kn[:, sl] + kslab).astype(BF16).reshape(nb, tt, HEAD_SLAB)
        v_ref[0] = lax.dot_general(wv_ref[...], lat_b, NT_DIMS, preferred_element_type=F32).astype(BF16)

    u = z[:, o_u:o_u + conv_ch] * jax.nn.sigmoid(z[:, o_u + conv_ch:o_u + 2 * conv_ch])
    u_ref[...] = u.reshape(nb, tt, conv_ch)


def _stage_in(x, cs, wts, *, nb, tt, absorb):
    b, t_len, d = x.shape
    c_q = wts["gq"].shape[-1]
    c_kv = wts["gkv"].shape[-1]
    conv_ch = wts["bconv"].shape[-1]
    n_gate = wts["wg"].shape[-1]
    grid = (b // nb, t_len // tt)

    def row_spec(width):
        return pl.BlockSpec((nb, tt, width), lambda i, j: (i, j, 0))

    in_specs = [row_spec(d), pl.BlockSpec((tt, 4 * HEAD_SLAB), lambda i, j: (j, 0))]
    names = ["gmix", "wa", "wg", "gq", "gkv", "wq", "wqr"]
    names += ["wukt"] if absorb else ["wkn", "wv"]
    in_specs += [_const_spec(wts[n].shape) for n in names]

    qk_w = N_HEADS * HEAD_SLAB
    out_shapes = [jax.ShapeDtypeStruct((b, t_len, qk_w), BF16)]
    out_specs = [row_spec(qk_w)]
    if absorb:
        out_shapes.append(jax.ShapeDtypeStruct((b, t_len, N_HEADS * c_kv), BF16))
        out_specs.append(row_spec(N_HEADS * c_kv))
    else:
        assert nb == 1, "transposed value tiles are written one sequence at a time"
        out_shapes += [jax.ShapeDtypeStruct((b, t_len, qk_w), BF16),
                       jax.ShapeDtypeStruct((b, N_HEADS * V_DIM, t_len), BF16)]
        out_specs += [row_spec(qk_w), pl.BlockSpec((1, N_HEADS * V_DIM, tt), lambda i, j: (i, 0, j))]
    out_shapes += [jax.ShapeDtypeStruct((b, t_len, c_kv), F32),
                   jax.ShapeDtypeStruct((b, t_len, QK_ROPE), F32),
                   jax.ShapeDtypeStruct((b, t_len, conv_ch), F32),
                   jax.ShapeDtypeStruct((b, t_len, n_gate), BF16)]
    out_specs += [row_spec(c_kv), row_spec(QK_ROPE), row_spec(conv_ch), row_spec(n_gate)]

    body = functools.partial(_stage_in_body, nb=nb, tt=tt, absorb=absorb, c_q=c_q, c_kv=c_kv,
                             conv_ch=conv_ch)
    return pl.pallas_call(
        body,
        grid=grid,
        in_specs=in_specs,
        out_specs=out_specs,
        out_shape=out_shapes,
        compiler_params=pltpu.CompilerParams(dimension_semantics=("arbitrary", "arbitrary"),
                                             vmem_limit_bytes=VMEM_LIMIT),
        name="stage_in_absorb" if absorb else "stage_in",
    )(x, cs, *[wts[n] for n in names])


def _attn_prompt_body(q_ref, k_ref, vt_ref, o_ref, sa_ref, sb_ref, m_ref, acc_ref, *, tq, qc, n_q):
    n_col = tq // qc
    slab = lambda hh: slice(hh * HEAD_SLAB, (hh + 1) * HEAD_SLAB)
    ones_rows = jnp.ones((ACC_ROWS - V_DIM, tq), BF16)

    def scores_into(kt, s_buf):
        for hh in range(2):
            k = k_ref[0, kt * tq:(kt + 1) * tq, slab(hh)]
            s_buf[hh] = lax.dot_general(k, q_ref[0, :, slab(hh)], NT_DIMS, preferred_element_type=F32)

    def update(c, s, vt, mask):
        if mask is not None:
            s = jnp.where(mask, s, NEG_INF)
        m = m_ref[c]
        m_new = jnp.maximum(m, jnp.max(s, axis=0, keepdims=True))
        p = jnp.exp2(s - m_new).astype(BF16)
        vt_aug = jnp.concatenate([vt, ones_rows[:, :vt.shape[1]]], axis=0)
        acc_ref[c] = jnp.exp2(m - m_new) * acc_ref[c] + _dot(vt_aug, p)
        m_ref[c] = m_new

    def full_step(kt, s_cur, s_nxt):
        scores_into(kt + 1, s_nxt)
        for hh in range(2):
            vt = vt_ref[0, hh * V_DIM:(hh + 1) * V_DIM, kt * tq:(kt + 1) * tq]
            for cb in range(n_col):
                update(hh * n_col + cb, s_cur[hh, :, cb * qc:(cb + 1) * qc], vt, None)

    def diag_step(qt, s_cur):
        for hh in range(2):
            for cb in range(n_col):
                c = hh * n_col + cb
                n_keys = (cb + 1) * qc
                vt = vt_ref[0, hh * V_DIM:(hh + 1) * V_DIM, qt * tq:qt * tq + n_keys]
                key_chunk = lax.broadcasted_iota(jnp.int32, (n_keys, qc), 0) // CHUNK
                qry_chunk = (lax.broadcasted_iota(jnp.int32, (n_keys, qc), 1) + cb * qc) // CHUNK
                update(c, s_cur[hh, :n_keys, cb * qc:(cb + 1) * qc], vt, key_chunk <= qry_chunk)
                acc = acc_ref[c]
                o_ref[0, hh * V_DIM:(hh + 1) * V_DIM, cb * qc:(cb + 1) * qc] = (
                    acc[:V_DIM] / acc[V_DIM:V_DIM + 1]).astype(BF16)

    bufs = (sa_ref, sb_ref)
    for qt in range(n_q):
        @pl.when(pl.program_id(2) == qt)
        def _(qt=qt):
            m_ref[...] = jnp.full(m_ref.shape, NEG_INF, F32)
            acc_ref[...] = jnp.zeros(acc_ref.shape, F32)
            scores_into(0, bufs[0])
            for kt in range(qt):
                full_step(kt, bufs[kt % 2], bufs[(kt + 1) % 2])
            diag_step(qt, bufs[qt % 2])


def _attn_prompt(q, k, vt, *, tq, qc):
    b, t_len, _ = q.shape
    grid = (b, N_HEADS // 2, t_len // tq)
    n_chains = 2 * (tq // qc)
    return pl.pallas_call(
        functools.partial(_attn_prompt_body, tq=tq, qc=qc, n_q=t_len // tq),
        grid=grid,
        in_specs=[pl.BlockSpec((1, tq, 2 * HEAD_SLAB), lambda i, p, j: (i, j, p)),
                  pl.BlockSpec((1, t_len, 2 * HEAD_SLAB), lambda i, p, j: (i, 0, p)),
                  pl.BlockSpec((1, 2 * V_DIM, t_len), lambda i, p, j: (i, p, 0))],
        out_specs=pl.BlockSpec((1, 2 * V_DIM, tq), lambda i, p, j: (i, p, j)),
        out_shape=jax.ShapeDtypeStruct((b, N_HEADS * V_DIM, t_len), BF16),
        scratch_shapes=[pltpu.VMEM((2, tq, tq), F32), pltpu.VMEM((2, tq, tq), F32),
                        pltpu.VMEM((n_chains, 1, qc), F32), pltpu.VMEM((n_chains, ACC_ROWS, qc), F32)],
        compiler_params=pltpu.CompilerParams(dimension_semantics=("arbitrary",) * 3,
                                             vmem_limit_bytes=VMEM_LIMIT),
        name="attn_prompt",
    )(q, k, vt)


def _attn_sample_body(q_ref, qlat_ref, clat_ref, ckrt_ref, nlat_ref, nkr_ref, o_ref, *, past, tt, c_kv):
    ql = jnp.concatenate([qlat_ref[0, :, hd * c_kv:(hd + 1) * c_kv] for hd in range(N_HEADS)], axis=0)
    qs = jnp.concatenate([q_ref[0, :, hd * HEAD_SLAB:(hd + 1) * HEAD_SLAB] for hd in range(N_HEADS)],
                         axis=0)
    qr = qs[:, QK_NOPE:QK_NOPE + QK_ROPE]

    clat = clat_ref[0].astype(BF16)
    ckrt = ckrt_ref[0].astype(BF16)
    nlat = nlat_ref[0].astype(BF16)
    nkr = nkr_ref[0].astype(BF16)
    s_c = (lax.dot_general(ql, clat, NT_DIMS, preferred_element_type=F32)
           + _dot(qr, ckrt)) * SM_SCALE
    s_n = (lax.dot_general(ql, nlat, NT_DIMS, preferred_element_type=F32)
           + lax.dot_general(qr, nkr, NT_DIMS, preferred_element_type=F32)) * SM_SCALE
    rows = N_HEADS * tt
    q_pos = past + lax.broadcasted_iota(jnp.int32, (rows, tt), 0) % tt
    k_pos = past + lax.broadcasted_iota(jnp.int32, (rows, tt), 1)
    s_n = jnp.where(k_pos // CHUNK <= q_pos // CHUNK, s_n, NEG_INF)

    m = jnp.maximum(jnp.max(s_c, axis=-1, keepdims=True), jnp.max(s_n, axis=-1, keepdims=True))
    p_c = jnp.exp(s_c - m)
    p_n = jnp.exp(s_n - m)
    l = jnp.sum(p_c, axis=-1, keepdims=True) + jnp.sum(p_n, axis=-1, keepdims=True)
    o = (_dot(p_c.astype(BF16), clat) + _dot(p_n.astype(BF16), nlat)) / l
    for hd in range(N_HEADS):
        o_ref[0, :, hd * c_kv:(hd + 1) * c_kv] = o[hd * tt:(hd + 1) * tt, :].astype(BF16)


def _attn_sample(q, qlat, clat, ckrt, nlat, nkr):
    b, tt, _ = q.shape
    past, c_kv = clat.shape[1], clat.shape[2]

    def spec(a):
        return pl.BlockSpec((1,) + a.shape[1:], lambda i: (i, 0, 0))

    return pl.pallas_call(
        functools.partial(_attn_sample_body, past=past, tt=tt, c_kv=c_kv),
        grid=(b,),
        in_specs=[spec(a) for a in (q, qlat, clat, ckrt, nlat, nkr)],
        out_specs=pl.BlockSpec((1, tt, N_HEADS * c_kv), lambda i: (i, 0, 0)),
        out_shape=jax.ShapeDtypeStruct((b, tt, N_HEADS * c_kv), BF16),
        compiler_params=pltpu.CompilerParams(dimension_semantics=("arbitrary",),
                                             vmem_limit_bytes=VMEM_LIMIT),
        name="attn_sample",
    )(q, qlat, clat, ckrt, nlat, nkr)


def _stage_out_body(x_ref, o_ref, u_ref, hist_ref, sg_ref, p_ref, bconv_ref, wconv_ref, lng_ref, lnb_ref,
                    wao_ref, wco_ref, wout_ref, gffn_ref, wup_ref, wdn_ref, gple_ref, wpg_ref, wpp_ref, gfin_ref,
                    *rest, nb, tt, tiles_per_seq, absorb, ff_chunk):
    if absorb:
        wuv_ref, y_ref, uext_ref, shift_ref = rest
    else:
        y_ref, uext_ref, shift_ref = rest
    t = pl.program_id(0) % tiles_per_seq
    x = x_ref[...]
    d = x.shape[-1]
    conv_ch = u_ref.shape[-1]

    @pl.when(t == 0)
    def _():
        uext_ref[:, :HIST_ROWS, :] = hist_ref[...]

    uext_ref[:, HIST_ROWS:, :] = u_ref[...]
    dw = jnp.zeros((nb, tt, conv_ch), F32) + bconv_ref[...][None]
    for r in range(SUBLANES):
        offs = [o for o in range(HIST_PAD, HIST_PAD + CONV_W) if o % SUBLANES == r]
        src_ref = uext_ref
        if r:
            n_rows = tt + max(offs) - r
            shift_ref[:, :n_rows, :] = uext_ref[:, r:r + n_rows, :]
            src_ref = shift_ref
        for o in offs:
            k = o - HIST_PAD
            dw = dw + src_ref[:, o - r:o - r + tt, :] * wconv_ref[k:k + 1, :][None]
    uext_ref[:, :HIST_ROWS, :] = uext_ref[:, tt:tt + HIST_ROWS, :]
    mu = jnp.mean(dw, axis=-1, keepdims=True)
    dc = dw - mu
    ln = dc * lax.rsqrt(jnp.mean(dc * dc, axis=-1, keepdims=True) + LN_EPS)
    ln = ln * lng_ref[...][None] + lnb_ref[...][None]
    cact = (ln * jax.nn.sigmoid(ln)).astype(BF16).reshape(nb * tt, conv_ch)

    if absorb:
        o = _dot(o_ref[...], wuv_ref[...]).astype(BF16)
        a = _dot(o, wao_ref[...])
    else:
        a = lax.dot_general(o_ref[0], wao_ref[...], TN_DIMS, preferred_element_type=F32)
    cb = _dot(cact, wco_ref[...])
    m = sg_ref[:, :d] * a + sg_ref[:, d:] * cb
    x = x + _dot(m.astype(BF16), wout_ref[...])

    h = _rms(x, gffn_ref[...]).astype(BF16)
    d_ff = wup_ref.shape[-1]
    ff = jnp.zeros_like(x)
    for c in range(d_ff // ff_chunk):
        sl = slice(c * ff_chunk, (c + 1) * ff_chunk)
        up = jnp.maximum(_dot(h, wup_ref[:, sl]), 0.0)
        ff = ff + _dot((up * up).astype(BF16), wdn_ref[sl, :])
    x = x + ff

    h = _rms(x, gple_ref[...]).astype(BF16)
    pg = jax.nn.sigmoid(_dot(h, wpg_ref[...]))
    x = x + pg * _dot(p_ref[...].astype(BF16), wpp_ref[...])
    y_ref[...] = _rms(x, gfin_ref[...])


def _stage_out(x, o, u, hist, sg, p, wts, *, nb, tt, absorb):
    n, d = x.shape
    b, t_len, conv_ch = u.shape
    tm = nb * tt
    tiles_per_seq = t_len // tt
    names = ["bconv", "wconv", "lng", "lnb", "wao", "wco", "wout", "gffn", "wup", "wdn", "gple", "wpg", "wpp",
             "gfin"]
    if absorb:
        names.append("wuv_bd")

    def row_spec(a):
        return pl.BlockSpec((tm, a.shape[-1]), lambda i: (i, 0))

    if absorb:
        o_spec = row_spec(o)
    else:
        o_spec = pl.BlockSpec((1, o.shape[1], tm), lambda i: (i // tiles_per_seq, 0, i % tiles_per_seq))
    u_spec = pl.BlockSpec((nb, tt, conv_ch), lambda i: (i // tiles_per_seq, i % tiles_per_seq, 0))
    hist_spec = pl.BlockSpec((nb, HIST_ROWS, conv_ch), lambda i: (i // tiles_per_seq, 0, 0))

    return pl.pallas_call(
        functools.partial(_stage_out_body, nb=nb, tt=tt, tiles_per_seq=tiles_per_seq, absorb=absorb,
                          ff_chunk=1024),
        grid=(n // tm,),
        in_specs=[row_spec(x), o_spec, u_spec, hist_spec, row_spec(sg), row_spec(p)]
        + [_const_spec(wts[k].shape) for k in names],
        out_specs=pl.BlockSpec((tm, d), lambda i: (i, 0)),
        out_shape=jax.ShapeDtypeStruct((n, d), F32),
        scratch_shapes=[pltpu.VMEM((nb, tt + HIST_ROWS, conv_ch), F32),
                        pltpu.VMEM((nb, tt + HIST_ROWS, conv_ch), F32)],
        compiler_params=pltpu.CompilerParams(dimension_semantics=("arbitrary",),
                                             vmem_limit_bytes=VMEM_LIMIT),
        name="stage_out_absorb" if absorb else "stage_out",
    )(x, o, u, hist, sg, p, *[wts[k] for k in names])


def _rot_half_cols(w):
    half = w.shape[-1] // 2
    return jnp.concatenate([-w[..., half:], w[..., :half]], axis=-1)


def _head_slabs(nope, rope):
    c, hds = (nope if nope is not None else rope).shape[:2]
    parts = [nope if nope is not None else jnp.zeros((c, hds, QK_NOPE), F32),
             rope if rope is not None else jnp.zeros((c, hds, QK_ROPE), F32),
             jnp.zeros((c, hds, HEAD_SLAB - QK_NOPE - QK_ROPE), F32)]
    return jnp.concatenate(parts, axis=-1).reshape(c, hds * HEAD_SLAB)


def _prep_weights(norm_mix_g, w_in, q_norm_g, w_uq, kv_norm_g, w_uk, w_uv, w_attn_out, conv_w, conv_b,
                  conv_ln_g, conv_ln_b, w_conv_out, w_out, norm_ffn_g, w_ff_up, w_ff_down, ple_norm_g,
                  w_ple_gate, w_ple_proj, final_norm_g):
    d = w_in.shape[0]
    c_q = q_norm_g.shape[-1]
    c_kv = kv_norm_g.shape[-1]
    conv_ch = conv_b.shape[-1]
    o_kv = c_q
    o_kr = o_kv + c_kv
    o_conv = o_kr + QK_ROPE
    o_gate = o_conv + 2 * conv_ch
    w_kr = w_in[:, o_kr:o_conv]
    zn = jnp.zeros((d, QK_NOPE), F32)
    zp = jnp.zeros((d, HEAD_SLAB - QK_NOPE - QK_ROPE), F32)
    wa = jnp.concatenate([w_in[:, :o_kr], zn, w_kr, zp, zn, _rot_half_cols(w_kr), zp,
                          w_in[:, o_conv:o_gate]], axis=1)
    q_nope, q_rope = w_uq[..., :QK_NOPE], w_uq[..., QK_NOPE:]
    row = lambda v: v.reshape(1, -1).astype(F32)
    wukt = jnp.transpose(w_uk, (1, 2, 0))
    wukt = jnp.concatenate([wukt, jnp.zeros((N_HEADS, HEAD_SLAB - QK_NOPE, c_kv), F32)], axis=1)
    eye = jnp.eye(N_HEADS, dtype=F32)
    wuv_bd = jnp.einsum("chd,hg->hcgd", w_uv, eye).reshape(N_HEADS * c_kv, N_HEADS * V_DIM)
    return {
        "gmix": row(norm_mix_g), "wa": wa.astype(BF16), "gq": row(q_norm_g), "gkv": row(kv_norm_g),
        "wq": _head_slabs(q_nope, q_rope).astype(BF16),
        "wqr": _head_slabs(None, _rot_half_cols(q_rope)).astype(BF16),
        "wkn": _head_slabs(w_uk, None).astype(BF16),
        "wv": w_uv.reshape(c_kv, N_HEADS * V_DIM).T.astype(BF16),
        "wukt": wukt.astype(BF16), "wuv_bd": wuv_bd.astype(BF16),
        "bconv": row(conv_b), "wconv": conv_w.astype(F32), "lng": row(conv_ln_g), "lnb": row(conv_ln_b),
        "wg": w_in[:, o_gate:].astype(BF16), "wao": w_attn_out.astype(BF16),
        "wco": w_conv_out.astype(BF16), "wout": w_out.astype(BF16), "gffn": row(norm_ffn_g),
        "wup": w_ff_up.astype(BF16), "wdn": w_ff_down.astype(BF16), "gple": row(ple_norm_g),
        "wpg": w_ple_gate.astype(BF16), "wpp": w_ple_proj.astype(BF16), "gfin": row(final_norm_g),
    }


def _rope_tables(pos, q_scale):
    half = QK_ROPE // 2
    inv = ROPE_THETA ** (-jnp.arange(half, dtype=F32) / half)
    ang = pos.astype(F32)[:, None] * inv[None, :]
    cos, sin = jnp.cos(ang), jnp.sin(ang)
    n = pos.shape[0]
    pad = jnp.zeros((n, HEAD_SLAB - QK_NOPE - QK_ROPE), F32)
    cos_t = jnp.concatenate([jnp.ones((n, QK_NOPE), F32), cos, cos, pad], axis=1)
    sin_t = jnp.concatenate([jnp.zeros((n, QK_NOPE), F32), sin, sin, pad], axis=1)
    return jnp.concatenate([cos_t * q_scale, sin_t * q_scale, cos_t, sin_t], axis=1)


def _conv_cache(hist, u):
    n_hist = CONV_W - 1
    t_len = u.shape[1]
    if t_len >= n_hist:
        return u[:, t_len - n_hist:]
    return jnp.concatenate([hist[:, HIST_ROWS - (n_hist - t_len):], u], axis=1)


def _tile(n, pref):
    return pref if n % pref == 0 else n


def kernel(x_prompt, x_sample, p_prompt, p_sample, cache_kv_latent, cache_k_rope, cache_conv, norm_mix_g, w_in, q_norm_g, w_uq, kv_norm_g, w_uk, w_uv, w_attn_out, conv_w, conv_b, conv_ln_g, conv_ln_b, w_conv_out, w_out, norm_ffn_g, w_ff_up, w_ff_down, ple_norm_g, w_ple_gate, w_ple_proj, final_norm_g):
    depth = w_in.shape[0]
    assert depth == 1, "one layer: the two request groups are independent within it"
    bp, tp, d = x_prompt.shape
    bs, ts, _ = x_sample.shape
    past = cache_kv_latent.shape[2]
    conv_ch = conv_b.shape[-1]
    wts = _prep_weights(norm_mix_g[0], w_in[0], q_norm_g[0], w_uq[0], kv_norm_g[0], w_uk[0], w_uv[0],
                        w_attn_out[0], conv_w[0], conv_b[0], conv_ln_g[0], conv_ln_b[0], w_conv_out[0],
                        w_out[0], norm_ffn_g[0], w_ff_up[0], w_ff_down[0], ple_norm_g[0], w_ple_gate[0],
                        w_ple_proj[0], final_norm_g)

    tt = _tile(tp, 512)
    hist0 = jnp.zeros((bp, HIST_ROWS, conv_ch), F32)
    q, k, vt, lat_p, kr_p, u_p, sg_p = _stage_in(x_prompt, _rope_tables(jnp.arange(tp), EXP2_SCALE), wts,
                                                  nb=1, tt=tt, absorb=False)
    o_p = _attn_prompt(q, k, vt, tq=ATTN_TQ, qc=ATTN_QC)
    n_p = bp * tp
    y_p = _stage_out(x_prompt.reshape(n_p, d), o_p, u_p, hist0, sg_p.reshape(n_p, -1),
                     p_prompt[0].reshape(n_p, -1), wts, nb=1, tt=tt, absorb=False)

    hist_s = jnp.pad(cache_conv[0], ((0, 0), (HIST_PAD, 0), (0, 0)))
    qs, qlat, lat_s, kr_s, u_s, sg_s = _stage_in(x_sample, _rope_tables(past + jnp.arange(ts), 1.0), wts,
                                                 nb=bs, tt=ts, absorb=True)
    o_s = _attn_sample(qs, qlat, cache_kv_latent[0], jnp.swapaxes(cache_k_rope[0], 1, 2), lat_s, kr_s)
    n_s = bs * ts
    y_s = _stage_out(x_sample.reshape(n_s, d), o_s.reshape(n_s, -1), u_s, hist_s, sg_s.reshape(n_s, -1),
                     p_sample[0].reshape(n_s, -1), wts, nb=bs, tt=ts, absorb=True)

    ncv_p = _conv_cache(hist0, u_p)
    ncv_s = _conv_cache(hist_s, u_s)

    return (y_p.reshape(bp, tp, d), y_s.reshape(bs, ts, d), lat_p[None], kr_p[None], ncv_p[None],
            lat_s[None], kr_s[None], ncv_s[None])
```

```python
import functools
import math

import jax
import jax.numpy as jnp
from jax import lax
from jax.experimental import pallas as pl
from jax.experimental.pallas import tpu as pltpu

F32 = jnp.float32
BF16 = jnp.bfloat16

CHUNK = 64
N_HEADS = 8
QK_NOPE = 64
QK_ROPE = 32
V_DIM = 64
CONV_W = 31
ROPE_THETA = 10000.0
RMS_EPS = 1e-6
LN_EPS = 1e-5
SM_SCALE = (QK_NOPE + QK_ROPE) ** -0.5
EXP2_SCALE = SM_SCALE * math.log2(math.e)
NEG_INF = -1e30

LANES = 128
SUBLANES = 8
HEAD_SLAB = LANES
HIST_ROWS = 32
HIST_PAD = HIST_ROWS - (CONV_W - 1)
ATTN_TQ = 1024
ATTN_QC = 256
SAMPLE_BB = 2
BF16_SUBLANES = 16
ACC_ROWS = V_DIM + BF16_SUBLANES
VMEM_LIMIT = 56 * 1024 * 1024


def _rms(x, g):
    return x * lax.rsqrt(jnp.mean(x * x, axis=-1, keepdims=True) + RMS_EPS) * g


def _dot(a, b):
    return jnp.dot(a, b, preferred_element_type=F32)


NT_DIMS = (((1,), (1,)), ((), ()))
TN_DIMS = (((0,), (0,)), ((), ()))


def _const_spec(shape):
    nd = len(shape)
    return pl.BlockSpec(shape, lambda *_: (0,) * nd, pipeline_mode=pl.Buffered(1))


def _stage_in_body(x_ref, cs_ref, gmix_ref, wa_ref, wg_ref, gq_ref, gkv_ref, wq_ref,
                   *rest, nb, tt, absorb, c_q, c_kv, conv_ch):
    if absorb:
        (wukt_ref,) = rest[:1]
        q_ref, qlat_ref, lat_ref, kr_ref, u_ref, sg_ref = rest[1:]
    else:
        wkn_ref, wv_ref = rest[:2]
        q_ref, k_ref, v_ref, lat_ref, kr_ref, u_ref, sg_ref = rest[2:]

    rows = nb * tt
    d = x_ref.shape[-1]

    x = x_ref[...].reshape(rows, d)
    h = _rms(x, gmix_ref[...]).astype(BF16)
    z = _dot(h, wa_ref[...])
    o_kv = c_q
    o_ks = o_kv + c_kv
    o_u = o_ks + HEAD_SLAB

    sg_ref[...] = jax.nn.sigmoid(_dot(h, wg_ref[...])).astype(BF16).reshape(nb, tt, sg_ref.shape[-1])

    cos_q, sin_q, cos_k, sin_k = (cs_ref[:, i * HEAD_SLAB:(i + 1) * HEAD_SLAB] for i in range(4))
    half = QK_ROPE // 2
    first_half = lax.broadcasted_iota(jnp.int32, (rows, HEAD_SLAB), 1) < QK_NOPE + half

    def rope3(a, cos, sin):
        swapped = jnp.where(first_half, pltpu.roll(a, HEAD_SLAB - half, axis=1), pltpu.roll(a, half, axis=1))
        a3 = a.reshape(nb, tt, HEAD_SLAB) * cos[None] + swapped.reshape(nb, tt, HEAD_SLAB) * sin[None]
        return a3.reshape(rows, HEAD_SLAB)

    cqn = _rms(z[:, :c_q], gq_ref[...]).astype(BF16)
    qa = _dot(cqn, wq_ref[...])
    q_heads = []
    for hd in range(N_HEADS):
        sl = slice(hd * HEAD_SLAB, (hd + 1) * HEAD_SLAB)
        qh = rope3(qa[:, sl], cos_q, sin_q).astype(BF16)
        q_heads.append(qh)
        q_ref[:, :, sl] = qh.reshape(nb, tt, HEAD_SLAB)

    latent = _rms(z[:, o_kv:o_ks], gkv_ref[...])
    lat_ref[...] = latent.reshape(nb, tt, c_kv)
    kslab = rope3(z[:, o_ks:o_u], cos_k, sin_k)
    kr_ref[...] = kslab[:, QK_NOPE:QK_NOPE + QK_ROPE].reshape(nb, tt, QK_ROPE)

    if absorb:
        for hd in range(N_HEADS):
            ql = _dot(q_heads[hd], wukt_ref[hd])
            qlat_ref[:, :, hd * c_kv:(hd + 1) * c_kv] = ql.astype(BF16).reshape(nb, tt, c_kv)
    else:
        lat_b = latent.astype(BF16)
        kn = _dot(lat_b, wkn_ref[...])
        for hd in range(N_HEADS):
            sl = slice(hd * HEAD_SLAB, (hd + 1) * HEAD_SLAB)
            k_ref[:, :, sl] = (kn[:, sl] + kslab).astype(BF16).reshape(nb, tt, HEAD_SLAB)
        v_ref[0] = lax.dot_general(wv_ref[...], lat_b, NT_DIMS, preferred_element_type=F32).astype(BF16)

    u = z[:, o_u:o_u + conv_ch] * jax.nn.sigmoid(z[:, o_u + conv_ch:o_u + 2 * conv_ch])
    u_ref[...] = u.reshape(nb, tt, conv_ch)


def _stage_in(x, cs, wts, *, nb, tt, absorb):
    b, t_len, d = x.shape
    c_q = wts["gq"].shape[-1]
    c_kv = wts["gkv"].shape[-1]
    conv_ch = wts["bconv"].shape[-1]
    n_gate = wts["wg"].shape[-1]
    grid = (b // nb, t_len // tt)

    def row_spec(width):
        return pl.BlockSpec((nb, tt, width), lambda i, j: (i, j, 0))

    in_specs = [row_spec(d), pl.BlockSpec((tt, 4 * HEAD_SLAB), lambda i, j: (j, 0))]
    names = ["gmix", "wa", "wg", "gq", "gkv", "wq"]
    names += ["wukt"] if absorb else ["wkn", "wv"]
    in_specs += [_const_spec(wts[n].shape) for n in names]

    qk_w = N_HEADS * HEAD_SLAB
    out_shapes = [jax.ShapeDtypeStruct((b, t_len, qk_w), BF16)]
    out_specs = [row_spec(qk_w)]
    if absorb:
        out_shapes.append(jax.ShapeDtypeStruct((b, t_len, N_HEADS * c_kv), BF16))
        out_specs.append(row_spec(N_HEADS * c_kv))
    else:
        assert nb == 1, "transposed value tiles are written one sequence at a time"
        out_shapes += [jax.ShapeDtypeStruct((b, t_len, qk_w), BF16),
                       jax.ShapeDtypeStruct((b, N_HEADS * V_DIM, t_len), BF16)]
        out_specs += [row_spec(qk_w), pl.BlockSpec((1, N_HEADS * V_DIM, tt), lambda i, j: (i, 0, j))]
    out_shapes += [jax.ShapeDtypeStruct((b, t_len, c_kv), F32),
                   jax.ShapeDtypeStruct((b, t_len, QK_ROPE), F32),
                   jax.ShapeDtypeStruct((b, t_len, conv_ch), F32),
                   jax.ShapeDtypeStruct((b, t_len, n_gate), BF16)]
    out_specs += [row_spec(c_kv), row_spec(QK_ROPE), row_spec(conv_ch), row_spec(n_gate)]

    body = functools.partial(_stage_in_body, nb=nb, tt=tt, absorb=absorb, c_q=c_q, c_kv=c_kv,
                             conv_ch=conv_ch)
    return pl.pallas_call(
        body,
        grid=grid,
        in_specs=in_specs,
        out_specs=out_specs,
        out_shape=out_shapes,
        compiler_params=pltpu.CompilerParams(dimension_semantics=("arbitrary", "arbitrary"),
                                             vmem_limit_bytes=VMEM_LIMIT),
        name="stage_in_absorb" if absorb else "stage_in",
    )(x, cs, *[wts[n] for n in names])


def _attn_prompt_body(q_ref, k_ref, vt_ref, o_ref, sa_ref, sb_ref, m_ref, acc_ref, *, tq, qc, n_q):
    n_col = tq // qc
    slab = lambda hh: slice(hh * HEAD_SLAB, (hh + 1) * HEAD_SLAB)
    ones_rows = jnp.ones((ACC_ROWS - V_DIM, tq), BF16)

    def scores_into(kt, s_buf):
        for hh in range(2):
            k = k_ref[0, kt * tq:(kt + 1) * tq, slab(hh)]
            s_buf[hh] = lax.dot_general(k, q_ref[0, :, slab(hh)], NT_DIMS, preferred_element_type=F32)

    def update(c, s, vt, mask):
        if mask is not None:
            s = jnp.where(mask, s, NEG_INF)
        m = m_ref[c]
        m_new = jnp.maximum(m, jnp.max(s, axis=0, keepdims=True))
        p = jnp.exp2(s - m_new).astype(BF16)
        vt_aug = jnp.concatenate([vt, ones_rows[:, :vt.shape[1]]], axis=0)
        acc_ref[c] = jnp.exp2(m - m_new) * acc_ref[c] + _dot(vt_aug, p)
        m_ref[c] = m_new

    def full_step(kt, s_cur, s_nxt):
        scores_into(kt + 1, s_nxt)
        for hh in range(2):
            vt = vt_ref[0, hh * V_DIM:(hh + 1) * V_DIM, kt * tq:(kt + 1) * tq]
            for cb in range(n_col):
                update(hh * n_col + cb, s_cur[hh, :, cb * qc:(cb + 1) * qc], vt, None)

    def diag_step(qt, s_cur):
        for hh in range(2):
            for cb in range(n_col):
                c = hh * n_col + cb
                n_keys = (cb + 1) * qc
                vt = vt_ref[0, hh * V_DIM:(hh + 1) * V_DIM, qt * tq:qt * tq + n_keys]
                key_chunk = lax.broadcasted_iota(jnp.int32, (n_keys, qc), 0) // CHUNK
                qry_chunk = (lax.broadcasted_iota(jnp.int32, (n_keys, qc), 1) + cb * qc) // CHUNK
                update(c, s_cur[hh, :n_keys, cb * qc:(cb + 1) * qc], vt, key_chunk <= qry_chunk)
                acc = acc_ref[c]
                o_ref[0, hh * V_DIM:(hh + 1) * V_DIM, cb * qc:(cb + 1) * qc] = (
                    acc[:V_DIM] / acc[V_DIM:V_DIM + 1]).astype(BF16)

    bufs = (sa_ref, sb_ref)
    for qt in range(n_q):
        @pl.when(pl.program_id(2) == qt)
        def _(qt=qt):
            m_ref[...] = jnp.full(m_ref.shape, NEG_INF, F32)
            acc_ref[...] = jnp.zeros(acc_ref.shape, F32)
            scores_into(0, bufs[0])
            for kt in range(qt):
                full_step(kt, bufs[kt % 2], bufs[(kt + 1) % 2])
            diag_step(qt, bufs[qt % 2])


def _attn_prompt(q, k, vt, *, tq, qc):
    b, t_len, _ = q.shape
    grid = (b, N_HEADS // 2, t_len // tq)
    n_chains = 2 * (tq // qc)
    return pl.pallas_call(
        functools.partial(_attn_prompt_body, tq=tq, qc=qc, n_q=t_len // tq),
        grid=grid,
        in_specs=[pl.BlockSpec((1, tq, 2 * HEAD_SLAB), lambda i, p, j: (i, j, p)),
                  pl.BlockSpec((1, t_len, 2 * HEAD_SLAB), lambda i, p, j: (i, 0, p)),
                  pl.BlockSpec((1, 2 * V_DIM, t_len), lambda i, p, j: (i, p, 0))],
        out_specs=pl.BlockSpec((1, 2 * V_DIM, tq), lambda i, p, j: (i, p, j)),
        out_shape=jax.ShapeDtypeStruct((b, N_HEADS * V_DIM, t_len), BF16),
        scratch_shapes=[pltpu.VMEM((2, tq, tq), F32), pltpu.VMEM((2, tq, tq), F32),
                        pltpu.VMEM((n_chains, 1, qc), F32), pltpu.VMEM((n_chains, ACC_ROWS, qc), F32)],
        compiler_params=pltpu.CompilerParams(dimension_semantics=("arbitrary",) * 3,
                                             vmem_limit_bytes=VMEM_LIMIT),
        name="attn_prompt",
    )(q, k, vt)


def _attn_sample_body(q_ref, qlat_ref, clat_ref, ckrt_ref, nlat_ref, nkr_ref, o_ref, *, past, tt, c_kv):
    for i in range(q_ref.shape[0]):
        _attn_sample_one(i, q_ref, qlat_ref, clat_ref, ckrt_ref, nlat_ref, nkr_ref, o_ref, past, tt, c_kv)


def _attn_sample_one(i, q_ref, qlat_ref, clat_ref, ckrt_ref, nlat_ref, nkr_ref, o_ref, past, tt, c_kv):
    ql = jnp.concatenate([qlat_ref[i, :, hd * c_kv:(hd + 1) * c_kv] for hd in range(N_HEADS)], axis=0)
    qs = jnp.concatenate([q_ref[i, :, hd * HEAD_SLAB:(hd + 1) * HEAD_SLAB] for hd in range(N_HEADS)],
                         axis=0)
    qr = qs[:, QK_NOPE:QK_NOPE + QK_ROPE]

    clat = clat_ref[i].astype(BF16)
    ckrt = ckrt_ref[i].astype(BF16)
    nlat = nlat_ref[i].astype(BF16)
    nkr = nkr_ref[i].astype(BF16)
    s_c = (lax.dot_general(ql, clat, NT_DIMS, preferred_element_type=F32)
           + _dot(qr, ckrt)) * SM_SCALE
    s_n = (lax.dot_general(ql, nlat, NT_DIMS, preferred_element_type=F32)
           + lax.dot_general(qr, nkr, NT_DIMS, preferred_element_type=F32)) * SM_SCALE
    rows = N_HEADS * tt
    q_pos = past + lax.broadcasted_iota(jnp.int32, (rows, tt), 0) % tt
    k_pos = past + lax.broadcasted_iota(jnp.int32, (rows, tt), 1)
    s_n = jnp.where(k_pos // CHUNK <= q_pos // CHUNK, s_n, NEG_INF)

    m = jnp.maximum(jnp.max(s_c, axis=-1, keepdims=True), jnp.max(s_n, axis=-1, keepdims=True))
    p_c = jnp.exp(s_c - m)
    p_n = jnp.exp(s_n - m)
    l = jnp.sum(p_c, axis=-1, keepdims=True) + jnp.sum(p_n, axis=-1, keepdims=True)
    o = (_dot(p_c.astype(BF16), clat) + _dot(p_n.astype(BF16), nlat)) / l
    for hd in range(N_HEADS):
        o_ref[i, :, hd * c_kv:(hd + 1) * c_kv] = o[hd * tt:(hd + 1) * tt, :].astype(BF16)


def _attn_sample(q, qlat, clat, ckrt, nlat, nkr):
    b, tt, _ = q.shape
    past, c_kv = clat.shape[1], clat.shape[2]
    bb = SAMPLE_BB if b % SAMPLE_BB == 0 else 1

    def spec(a):
        return pl.BlockSpec((bb,) + a.shape[1:], lambda i: (i, 0, 0))

    return pl.pallas_call(
        functools.partial(_attn_sample_body, past=past, tt=tt, c_kv=c_kv),
        grid=(b // bb,),
        in_specs=[spec(a) for a in (q, qlat, clat, ckrt, nlat, nkr)],
        out_specs=pl.BlockSpec((bb, tt, N_HEADS * c_kv), lambda i: (i, 0, 0)),
        out_shape=jax.ShapeDtypeStruct((b, tt, N_HEADS * c_kv), BF16),
        compiler_params=pltpu.CompilerParams(dimension_semantics=("arbitrary",),
                                             vmem_limit_bytes=VMEM_LIMIT),
        name="attn_sample",
    )(q, qlat, clat, ckrt, nlat, nkr)


def _stage_out_body(x_ref, o_ref, u_ref, hist_ref, sg_ref, p_ref, bconv_ref, wconv_ref, lng_ref, lnb_ref,
                    wao_ref, wco_ref, wout_ref, gffn_ref, wup_ref, wdn_ref, gple_ref, wpg_ref, wpp_ref, gfin_ref,
                    *rest, nb, tt, tiles_per_seq, absorb, ff_chunk):
    if absorb:
        wuv_ref, y_ref, uext_ref, shift_ref = rest
    else:
        y_ref, uext_ref, shift_ref = rest
    t = pl.program_id(0) % tiles_per_seq
    x = x_ref[...]
    d = x.shape[-1]
    conv_ch = u_ref.shape[-1]

    @pl.when(t == 0)
    def _():
        uext_ref[:, :HIST_ROWS, :] = hist_ref[...]

    uext_ref[:, HIST_ROWS:, :] = u_ref[...]
    dw = jnp.zeros((nb, tt, conv_ch), F32) + bconv_ref[...][None]
    for r in range(SUBLANES):
        offs = [o for o in range(HIST_PAD, HIST_PAD + CONV_W) if o % SUBLANES == r]
        src_ref = uext_ref
        if r:
            n_rows = tt + max(offs) - r
            shift_ref[:, :n_rows, :] = uext_ref[:, r:r + n_rows, :]
            src_ref = shift_ref
        for o in offs:
            k = o - HIST_PAD
            dw = dw + src_ref[:, o - r:o - r + tt, :] * wconv_ref[k:k + 1, :][None]
    uext_ref[:, :HIST_ROWS, :] = uext_ref[:, tt:tt + HIST_ROWS, :]
    mu = jnp.mean(dw, axis=-1, keepdims=True)
    dc = dw - mu
    ln = dc * lax.rsqrt(jnp.mean(dc * dc, axis=-1, keepdims=True) + LN_EPS)
    ln = ln * lng_ref[...][None] + lnb_ref[...][None]
    cact = (ln * jax.nn.sigmoid(ln)).astype(BF16).reshape(nb * tt, conv_ch)

    if absorb:
        o = _dot(o_ref[...], wuv_ref[...]).astype(BF16)
        a = _dot(o, wao_ref[...])
    else:
        a = lax.dot_general(o_ref[0], wao_ref[...], TN_DIMS, preferred_element_type=F32)
    cb = _dot(cact, wco_ref[...])
    m = sg_ref[:, :d] * a + sg_ref[:, d:] * cb
    x = x + _dot(m.astype(BF16), wout_ref[...])

    h = _rms(x, gffn_ref[...]).astype(BF16)
    d_ff = wup_ref.shape[-1]
    ff = jnp.zeros_like(x)
    for c in range(d_ff // ff_chunk):
        sl = slice(c * ff_chunk, (c + 1) * ff_chunk)
        up = jnp.maximum(_dot(h, wup_ref[:, sl]), 0.0)
        ff = ff + _dot((up * up).astype(BF16), wdn_ref[sl, :])
    x = x + ff

    h = _rms(x, gple_ref[...]).astype(BF16)
    pg = jax.nn.sigmoid(_dot(h, wpg_ref[...]))
    x = x + pg * _dot(p_ref[...].astype(BF16), wpp_ref[...])
    y_ref[...] = _rms(x, gfin_ref[...])


def _stage_out(x, o, u, hist, sg, p, wts, *, nb, tt, absorb):
    n, d = x.shape
    b, t_len, conv_ch = u.shape
    tm = nb * tt
    tiles_per_seq = t_len // tt
    names = ["bconv", "wconv", "lng", "lnb", "wao", "wco", "wout", "gffn", "wup", "wdn", "gple", "wpg", "wpp",
             "gfin"]
    if absorb:
        names.append("wuv_bd")

    def row_spec(a):
        return pl.BlockSpec((tm, a.shape[-1]), lambda i: (i, 0))

    if absorb:
        o_spec = row_spec(o)
    else:
        o_spec = pl.BlockSpec((1, o.shape[1], tm), lambda i: (i // tiles_per_seq, 0, i % tiles_per_seq))
    u_spec = pl.BlockSpec((nb, tt, conv_ch), lambda i: (i // tiles_per_seq, i % tiles_per_seq, 0))
    hist_spec = pl.BlockSpec((nb, HIST_ROWS, conv_ch), lambda i: (i // tiles_per_seq, 0, 0))

    return pl.pallas_call(
        functools.partial(_stage_out_body, nb=nb, tt=tt, tiles_per_seq=tiles_per_seq, absorb=absorb,
                          ff_chunk=1024),
        grid=(n // tm,),
        in_specs=[row_spec(x), o_spec, u_spec, hist_spec, row_spec(sg), row_spec(p)]
        + [_const_spec(wts[k].shape) for k in names],
        out_specs=pl.BlockSpec((tm, d), lambda i: (i, 0)),
        out_shape=jax.ShapeDtypeStruct((n, d), F32),
        scratch_shapes=[pltpu.VMEM((nb, tt + HIST_ROWS, conv_ch), F32),
                        pltpu.VMEM((nb, tt + HIST_ROWS, conv_ch), F32)],
        compiler_params=pltpu.CompilerParams(dimension_semantics=("arbitrary",),
                                             vmem_limit_bytes=VMEM_LIMIT),
        name="stage_out_absorb" if absorb else "stage_out",
    )(x, o, u, hist, sg, p, *[wts[k] for k in names])


def _head_slabs(nope, rope):
    c, hds = (nope if nope is not None else rope).shape[:2]
    parts = [nope if nope is not None else jnp.zeros((c, hds, QK_NOPE), F32),
             rope if rope is not None else jnp.zeros((c, hds, QK_ROPE), F32),
             jnp.zeros((c, hds, HEAD_SLAB - QK_NOPE - QK_ROPE), F32)]
    return jnp.concatenate(parts, axis=-1).reshape(c, hds * HEAD_SLAB)


def _prep_weights(norm_mix_g, w_in, q_norm_g, w_uq, kv_norm_g, w_uk, w_uv, w_attn_out, conv_w, conv_b,
                  conv_ln_g, conv_ln_b, w_conv_out, w_out, norm_ffn_g, w_ff_up, w_ff_down, ple_norm_g,
                  w_ple_gate, w_ple_proj, final_norm_g):
    d = w_in.shape[0]
    c_q = q_norm_g.shape[-1]
    c_kv = kv_norm_g.shape[-1]
    conv_ch = conv_b.shape[-1]
    o_kv = c_q
    o_kr = o_kv + c_kv
    o_conv = o_kr + QK_ROPE
    o_gate = o_conv + 2 * conv_ch
    w_kr = w_in[:, o_kr:o_conv]
    zn = jnp.zeros((d, QK_NOPE), F32)
    zp = jnp.zeros((d, HEAD_SLAB - QK_NOPE - QK_ROPE), F32)
    wa = jnp.concatenate([w_in[:, :o_kr], zn, w_kr, zp, w_in[:, o_conv:o_gate]], axis=1)
    q_nope, q_rope = w_uq[..., :QK_NOPE], w_uq[..., QK_NOPE:]
    row = lambda v: v.reshape(1, -1).astype(F32)
    wukt = jnp.transpose(w_uk, (1, 2, 0))
    wukt = jnp.concatenate([wukt, jnp.zeros((N_HEADS, HEAD_SLAB - QK_NOPE, c_kv), F32)], axis=1)
    eye = jnp.eye(N_HEADS, dtype=F32)
    wuv_bd = jnp.einsum("chd,hg->hcgd", w_uv, eye).reshape(N_HEADS * c_kv, N_HEADS * V_DIM)
    return {
        "gmix": row(norm_mix_g), "wa": wa.astype(BF16), "gq": row(q_norm_g), "gkv": row(kv_norm_g),
        "wq": _head_slabs(q_nope, q_rope).astype(BF16),
        "wkn": _head_slabs(w_uk, None).astype(BF16),
        "wv": w_uv.reshape(c_kv, N_HEADS * V_DIM).T.astype(BF16),
        "wukt": wukt.astype(BF16), "wuv_bd": wuv_bd.astype(BF16),
        "bconv": row(conv_b), "wconv": conv_w.astype(F32), "lng": row(conv_ln_g), "lnb": row(conv_ln_b),
        "wg": w_in[:, o_gate:].astype(BF16), "wao": w_attn_out.astype(BF16),
        "wco": w_conv_out.astype(BF16), "wout": w_out.astype(BF16), "gffn": row(norm_ffn_g),
        "wup": w_ff_up.astype(BF16), "wdn": w_ff_down.astype(BF16), "gple": row(ple_norm_g),
        "wpg": w_ple_gate.astype(BF16), "wpp": w_ple_proj.astype(BF16), "gfin": row(final_norm_g),
    }


def _rope_tables(pos, q_scale):
    half = QK_ROPE // 2
    inv = ROPE_THETA ** (-jnp.arange(half, dtype=F32) / half)
    ang = pos.astype(F32)[:, None] * inv[None, :]
    cos, sin = jnp.cos(ang), jnp.sin(ang)
    n = pos.shape[0]
    pad = jnp.zeros((n, HEAD_SLAB - QK_NOPE - QK_ROPE), F32)
    cos_t = jnp.concatenate([jnp.ones((n, QK_NOPE), F32), cos, cos, pad], axis=1)
    sin_t = jnp.concatenate([jnp.zeros((n, QK_NOPE), F32), -sin, sin, pad], axis=1)
    return jnp.concatenate([cos_t * q_scale, sin_t * q_scale, cos_t, sin_t], axis=1)


def _conv_cache(hist, u):
    n_hist = CONV_W - 1
    t_len = u.shape[1]
    if t_len >= n_hist:
        return u[:, t_len - n_hist:]
    return jnp.concatenate([hist[:, HIST_ROWS - (n_hist - t_len):], u], axis=1)


def _tile(n, pref):
    return pref if n % pref == 0 else n


def kernel(x_prompt, x_sample, p_prompt, p_sample, cache_kv_latent, cache_k_rope, cache_conv, norm_mix_g, w_in, q_norm_g, w_uq, kv_norm_g, w_uk, w_uv, w_attn_out, conv_w, conv_b, conv_ln_g, conv_ln_b, w_conv_out, w_out, norm_ffn_g, w_ff_up, w_ff_down, ple_norm_g, w_ple_gate, w_ple_proj, final_norm_g):
    depth = w_in.shape[0]
    assert depth == 1, "one layer: the two request groups are independent within it"
    bp, tp, d = x_prompt.shape
    bs, ts, _ = x_sample.shape
    past = cache_kv_latent.shape[2]
    conv_ch = conv_b.shape[-1]
    wts = _prep_weights(norm_mix_g[0], w_in[0], q_norm_g[0], w_uq[0], kv_norm_g[0], w_uk[0], w_uv[0],
                        w_attn_out[0], conv_w[0], conv_b[0], conv_ln_g[0], conv_ln_b[0], w_conv_out[0],
                        w_out[0], norm_ffn_g[0], w_ff_up[0], w_ff_down[0], ple_norm_g[0], w_ple_gate[0],
                        w_ple_proj[0], final_norm_g)

    tt = _tile(tp, 512)
    hist0 = jnp.zeros((bp, HIST_ROWS, conv_ch), F32)
    q, k, vt, lat_p, kr_p, u_p, sg_p = _stage_in(x_prompt, _rope_tables(jnp.arange(tp), EXP2_SCALE), wts,
                                                  nb=1, tt=tt, absorb=False)
    o_p = _attn_prompt(q, k, vt, tq=ATTN_TQ, qc=ATTN_QC)
    n_p = bp * tp
    y_p = _stage_out(x_prompt.reshape(n_p, d), o_p, u_p, hist0, sg_p.reshape(n_p, -1),
                     p_prompt[0].reshape(n_p, -1), wts, nb=1, tt=tt, absorb=False)

    hist_s = jnp.pad(cache_conv[0], ((0, 0), (HIST_PAD, 0), (0, 0)))
    qs, qlat, lat_s, kr_s, u_s, sg_s = _stage_in(x_sample, _rope_tables(past + jnp.arange(ts), 1.0), wts,
                                                 nb=bs, tt=ts, absorb=True)
    o_s = _attn_sample(qs, qlat, cache_kv_latent[0], jnp.swapaxes(cache_k_rope[0], 1, 2), lat_s, kr_s)
    n_s = bs * ts
    y_s = _stage_out(x_sample.reshape(n_s, d), o_s.reshape(n_s, -1), u_s, hist_s, sg_s.reshape(n_s, -1),
                     p_sample[0].reshape(n_s, -1), wts, nb=bs, tt=ts, absorb=True)

    ncv_p = _conv_cache(hist0, u_p)
    ncv_s = _conv_cache(hist_s, u_s)

    return (y_p.reshape(bp, tp, d), y_s.reshape(bs, ts, d), lat_p[None], kr_p[None], ncv_p[None],
            lat_s[None], kr_s[None], ncv_s[None])
```

```python
import functools
import math

import jax
import jax.numpy as jnp
from jax import lax
from jax.experimental import pallas as pl
from jax.experimental.pallas import tpu as pltpu

F32 = jnp.float32
BF16 = jnp.bfloat16

CHUNK = 64
N_HEADS = 8
QK_NOPE = 64
QK_ROPE = 32
V_DIM = 64
CONV_W = 31
ROPE_THETA = 10000.0
RMS_EPS = 1e-6
LN_EPS = 1e-5
SM_SCALE = (QK_NOPE + QK_ROPE) ** -0.5
EXP2_SCALE = SM_SCALE * math.log2(math.e)
NEG_INF = -1e30

LANES = 128
SUBLANES = 8
HEAD_SLAB = LANES
HIST_ROWS = 32
HIST_PAD = HIST_ROWS - (CONV_W - 1)
ATTN_TQ = 1024
ATTN_QC = 256
SAMPLE_BB = 2
BF16_SUBLANES = 16
ACC_ROWS = V_DIM + BF16_SUBLANES
VMEM_LIMIT = 56 * 1024 * 1024


def _rms(x, g):
    return x * lax.rsqrt(jnp.mean(x * x, axis=-1, keepdims=True) + RMS_EPS) * g


def _dot(a, b):
    return jnp.dot(a, b, preferred_element_type=F32)


NT_DIMS = (((1,), (1,)), ((), ()))
TN_DIMS = (((0,), (0,)), ((), ()))


def _const_spec(shape):
    nd = len(shape)
    return pl.BlockSpec(shape, lambda *_: (0,) * nd, pipeline_mode=pl.Buffered(1))


def _stage_in_body(x_ref, cs_ref, gmix_ref, wa_ref, wg_ref, gq_ref, gkv_ref, wq_ref,
                   *rest, nb, tt, absorb, c_q, c_kv, conv_ch):
    if absorb:
        (wukt_ref,) = rest[:1]
        q_ref, qlat_ref, lat_ref, kr_ref, u_ref, sg_ref = rest[1:]
    else:
        wkn_ref, wv_ref = rest[:2]
        q_ref, k_ref, v_ref, lat_ref, kr_ref, u_ref, sg_ref = rest[2:]

    rows = nb * tt
    d = x_ref.shape[-1]

    x = x_ref[...].reshape(rows, d)
    h = _rms(x, gmix_ref[...]).astype(BF16)
    z = _dot(h, wa_ref[...])
    o_kv = c_q
    o_ks = o_kv + c_kv
    o_u = o_ks + HEAD_SLAB

    sg_ref[...] = jax.nn.sigmoid(_dot(h, wg_ref[...])).astype(BF16).reshape(nb, tt, sg_ref.shape[-1])

    cos_q, sin_q, cos_k, sin_k = (cs_ref[:, i * HEAD_SLAB:(i + 1) * HEAD_SLAB] for i in range(4))
    half = QK_ROPE // 2
    first_half = lax.broadcasted_iota(jnp.int32, (rows, HEAD_SLAB), 1) < QK_NOPE + half

    def rope3(a, cos, sin):
        swapped = jnp.where(first_half, pltpu.roll(a, HEAD_SLAB - half, axis=1), pltpu.roll(a, half, axis=1))
        a3 = a.reshape(nb, tt, HEAD_SLAB) * cos[None] + swapped.reshape(nb, tt, HEAD_SLAB) * sin[None]
        return a3.reshape(rows, HEAD_SLAB)

    cqn = _rms(z[:, :c_q], gq_ref[...]).astype(BF16)
    qa = _dot(cqn, wq_ref[...])
    q_heads = []
    for hd in range(N_HEADS):
        sl = slice(hd * HEAD_SLAB, (hd + 1) * HEAD_SLAB)
        qh = rope3(qa[:, sl], cos_q, sin_q).astype(BF16)
        q_heads.append(qh)
        q_ref[:, :, sl] = qh.reshape(nb, tt, HEAD_SLAB)

    latent = _rms(z[:, o_kv:o_ks], gkv_ref[...])
    lat_ref[...] = latent.reshape(nb, tt, c_kv)
    kslab = rope3(z[:, o_ks:o_u], cos_k, sin_k)
    kr_ref[...] = kslab[:, QK_NOPE:QK_NOPE + QK_ROPE].reshape(nb, tt, QK_ROPE)

    if absorb:
        for hd in range(N_HEADS):
            ql = _dot(q_heads[hd], wukt_ref[hd])
            qlat_ref[:, :, hd * c_kv:(hd + 1) * c_kv] = ql.astype(BF16).reshape(nb, tt, c_kv)
    else:
        lat_b = latent.astype(BF16)
        kn = _dot(lat_b, wkn_ref[...])
        for hd in range(N_HEADS):
            sl = slice(hd * HEAD_SLAB, (hd + 1) * HEAD_SLAB)
            k_ref[:, :, sl] = (kn[:, sl] + kslab).astype(BF16).reshape(nb, tt, HEAD_SLAB)
        v_ref[0] = lax.dot_general(wv_ref[...], lat_b, NT_DIMS, preferred_element_type=F32).astype(BF16)

    u = z[:, o_u:o_u + conv_ch] * jax.nn.sigmoid(z[:, o_u + conv_ch:o_u + 2 * conv_ch])
    u_ref[...] = u.reshape(nb, tt, conv_ch)


def _stage_in(x, cs, wts, *, nb, tt, absorb):
    b, t_len, d = x.shape
    c_q = wts["gq"].shape[-1]
    c_kv = wts["gkv"].shape[-1]
    conv_ch = wts["bconv"].shape[-1]
    n_gate = wts["wg"].shape[-1]
    grid = (b // nb, t_len // tt)

    def row_spec(width):
        return pl.BlockSpec((nb, tt, width), lambda i, j: (i, j, 0))

    in_specs = [row_spec(d), pl.BlockSpec((tt, 4 * HEAD_SLAB), lambda i, j: (j, 0))]
    names = ["gmix", "wa", "wg", "gq", "gkv", "wq"]
    names += ["wukt"] if absorb else ["wkn", "wv"]
    in_specs += [_const_spec(wts[n].shape) for n in names]

    qk_w = N_HEADS * HEAD_SLAB
    out_shapes = [jax.ShapeDtypeStruct((b, t_len, qk_w), BF16)]
    out_specs = [row_spec(qk_w)]
    if absorb:
        out_shapes.append(jax.ShapeDtypeStruct((b, t_len, N_HEADS * c_kv), BF16))
        out_specs.append(row_spec(N_HEADS * c_kv))
    else:
        assert nb == 1, "transposed value tiles are written one sequence at a time"
        out_shapes += [jax.ShapeDtypeStruct((b, t_len, qk_w), BF16),
                       jax.ShapeDtypeStruct((b, N_HEADS * V_DIM, t_len), BF16)]
        out_specs += [row_spec(qk_w), pl.BlockSpec((1, N_HEADS * V_DIM, tt), lambda i, j: (i, 0, j))]
    out_shapes += [jax.ShapeDtypeStruct((b, t_len, c_kv), F32),
                   jax.ShapeDtypeStruct((b, t_len, QK_ROPE), F32),
                   jax.ShapeDtypeStruct((b, t_len, conv_ch), F32),
                   jax.ShapeDtypeStruct((b, t_len, n_gate), BF16)]
    out_specs += [row_spec(c_kv), row_spec(QK_ROPE), row_spec(conv_ch), row_spec(n_gate)]

    body = functools.partial(_stage_in_body, nb=nb, tt=tt, absorb=absorb, c_q=c_q, c_kv=c_kv,
                             conv_ch=conv_ch)
    return pl.pallas_call(
        body,
        grid=grid,
        in_specs=in_specs,
        out_specs=out_specs,
        out_shape=out_shapes,
        compiler_params=pltpu.CompilerParams(dimension_semantics=("arbitrary", "arbitrary"),
                                             vmem_limit_bytes=VMEM_LIMIT),
        name="stage_in_absorb" if absorb else "stage_in",
    )(x, cs, *[wts[n] for n in names])


def _attn_prompt_body(q_ref, k_ref, vt_ref, o_ref, sa_ref, sb_ref, m_ref, acc_ref, *, tq, qc, n_q):
    n_col = tq // qc
    slab = lambda hh: slice(hh * HEAD_SLAB, (hh + 1) * HEAD_SLAB)
    ones_rows = jnp.ones((ACC_ROWS - V_DIM, tq), BF16)

    def scores_into(kt, s_buf, diag):
        for hh in range(2):
            if not diag:
                k = k_ref[0, kt * tq:(kt + 1) * tq, slab(hh)]
                s_buf[hh] = lax.dot_general(k, q_ref[0, :, slab(hh)], NT_DIMS, preferred_element_type=F32)
                continue
            for cb in range(n_col):
                n_keys = (cb + 1) * qc
                k = k_ref[0, kt * tq:kt * tq + n_keys, slab(hh)]
                s_buf[hh, :n_keys, cb * qc:(cb + 1) * qc] = lax.dot_general(
                    k, q_ref[0, cb * qc:(cb + 1) * qc, slab(hh)], NT_DIMS, preferred_element_type=F32)

    def update(c, s, vt, mask):
        if mask is not None:
            s = jnp.where(mask, s, NEG_INF)
        m = m_ref[c]
        m_new = jnp.maximum(m, jnp.max(s, axis=0, keepdims=True))
        p = jnp.exp2(s - m_new).astype(BF16)
        vt_aug = jnp.concatenate([vt, ones_rows[:, :vt.shape[1]]], axis=0)
        acc_ref[c] = jnp.exp2(m - m_new) * acc_ref[c] + _dot(vt_aug, p)
        m_ref[c] = m_new

    def full_step(kt, s_cur, s_nxt, next_is_diag):
        scores_into(kt + 1, s_nxt, next_is_diag)
        for hh in range(2):
            vt = vt_ref[0, hh * V_DIM:(hh + 1) * V_DIM, kt * tq:(kt + 1) * tq]
            for cb in range(n_col):
                update(hh * n_col + cb, s_cur[hh, :, cb * qc:(cb + 1) * qc], vt, None)

    def diag_step(qt, s_cur):
        for hh in range(2):
            for cb in range(n_col):
                c = hh * n_col + cb
                n_keys = (cb + 1) * qc
                vt = vt_ref[0, hh * V_DIM:(hh + 1) * V_DIM, qt * tq:qt * tq + n_keys]
                key_chunk = lax.broadcasted_iota(jnp.int32, (n_keys, qc), 0) // CHUNK
                qry_chunk = (lax.broadcasted_iota(jnp.int32, (n_keys, qc), 1) + cb * qc) // CHUNK
                update(c, s_cur[hh, :n_keys, cb * qc:(cb + 1) * qc], vt, key_chunk <= qry_chunk)
                acc = acc_ref[c]
                o_ref[0, hh * V_DIM:(hh + 1) * V_DIM, cb * qc:(cb + 1) * qc] = (
                    acc[:V_DIM] / acc[V_DIM:V_DIM + 1]).astype(BF16)

    bufs = (sa_ref, sb_ref)
    for qt in range(n_q):
        @pl.when(pl.program_id(2) == qt)
        def _(qt=qt):
            m_ref[...] = jnp.full(m_ref.shape, NEG_INF, F32)
            acc_ref[...] = jnp.zeros(acc_ref.shape, F32)
            scores_into(0, bufs[0], qt == 0)
            for kt in range(qt):
                full_step(kt, bufs[kt % 2], bufs[(kt + 1) % 2], kt + 1 == qt)
            diag_step(qt, bufs[qt % 2])


def _attn_prompt(q, k, vt, *, tq, qc):
    b, t_len, _ = q.shape
    grid = (b, N_HEADS // 2, t_len // tq)
    n_chains = 2 * (tq // qc)
    return pl.pallas_call(
        functools.partial(_attn_prompt_body, tq=tq, qc=qc, n_q=t_len // tq),
        grid=grid,
        in_specs=[pl.BlockSpec((1, tq, 2 * HEAD_SLAB), lambda i, p, j: (i, j, p)),
                  pl.BlockSpec((1, t_len, 2 * HEAD_SLAB), lambda i, p, j: (i, 0, p)),
                  pl.BlockSpec((1, 2 * V_DIM, t_len), lambda i, p, j: (i, p, 0))],
        out_specs=pl.BlockSpec((1, 2 * V_DIM, tq), lambda i, p, j: (i, p, j)),
        out_shape=jax.ShapeDtypeStruct((b, N_HEADS * V_DIM, t_len), BF16),
        scratch_shapes=[pltpu.VMEM((2, tq, tq), F32), pltpu.VMEM((2, tq, tq), F32),
                        pltpu.VMEM((n_chains, 1, qc), F32), pltpu.VMEM((n_chains, ACC_ROWS, qc), F32)],
        compiler_params=pltpu.CompilerParams(dimension_semantics=("arbitrary",) * 3,
                                             vmem_limit_bytes=VMEM_LIMIT),
        name="attn_prompt",
    )(q, k, vt)


def _attn_sample_body(q_ref, qlat_ref, clat_ref, ckrt_ref, nlat_ref, nkr_ref, o_ref, *, past, tt, c_kv):
    for i in range(q_ref.shape[0]):
        _attn_sample_one(i, q_ref, qlat_ref, clat_ref, ckrt_ref, nlat_ref, nkr_ref, o_ref, past, tt, c_kv)


def _attn_sample_one(i, q_ref, qlat_ref, clat_ref, ckrt_ref, nlat_ref, nkr_ref, o_ref, past, tt, c_kv):
    ql = jnp.concatenate([qlat_ref[i, :, hd * c_kv:(hd + 1) * c_kv] for hd in range(N_HEADS)], axis=0)
    qs = jnp.concatenate([q_ref[i, :, hd * HEAD_SLAB:(hd + 1) * HEAD_SLAB] for hd in range(N_HEADS)],
                         axis=0)
    qr = qs[:, QK_NOPE:QK_NOPE + QK_ROPE]

    clat = clat_ref[i].astype(BF16)
    ckrt = ckrt_ref[i].astype(BF16)
    nlat = nlat_ref[i].astype(BF16)
    nkr = nkr_ref[i].astype(BF16)
    s_c = (lax.dot_general(ql, clat, NT_DIMS, preferred_element_type=F32)
           + _dot(qr, ckrt)) * SM_SCALE
    s_n = (lax.dot_general(ql, nlat, NT_DIMS, preferred_element_type=F32)
           + lax.dot_general(qr, nkr, NT_DIMS, preferred_element_type=F32)) * SM_SCALE
    rows = N_HEADS * tt
    q_pos = past + lax.broadcasted_iota(jnp.int32, (rows, tt), 0) % tt
    k_pos = past + lax.broadcasted_iota(jnp.int32, (rows, tt), 1)
    s_n = jnp.where(k_pos // CHUNK <= q_pos // CHUNK, s_n, NEG_INF)

    m = jnp.maximum(jnp.max(s_c, axis=-1, keepdims=True), jnp.max(s_n, axis=-1, keepdims=True))
    p_c = jnp.exp(s_c - m)
    p_n = jnp.exp(s_n - m)
    l = jnp.sum(p_c, axis=-1, keepdims=True) + jnp.sum(p_n, axis=-1, keepdims=True)
    o = (_dot(p_c.astype(BF16), clat) + _dot(p_n.astype(BF16), nlat)) / l
    for hd in range(N_HEADS):
        o_ref[i, :, hd * c_kv:(hd + 1) * c_kv] = o[hd * tt:(hd + 1) * tt, :].astype(BF16)


def _attn_sample(q, qlat, clat, ckrt, nlat, nkr):
    b, tt, _ = q.shape
    past, c_kv = clat.shape[1], clat.shape[2]
    bb = SAMPLE_BB if b % SAMPLE_BB == 0 else 1

    def spec(a):
        return pl.BlockSpec((bb,) + a.shape[1:], lambda i: (i, 0, 0))

    return pl.pallas_call(
        functools.partial(_attn_sample_body, past=past, tt=tt, c_kv=c_kv),
        grid=(b // bb,),
        in_specs=[spec(a) for a in (q, qlat, clat, ckrt, nlat, nkr)],
        out_specs=pl.BlockSpec((bb, tt, N_HEADS * c_kv), lambda i: (i, 0, 0)),
        out_shape=jax.ShapeDtypeStruct((b, tt, N_HEADS * c_kv), BF16),
        compiler_params=pltpu.CompilerParams(dimension_semantics=("arbitrary",),
                                             vmem_limit_bytes=VMEM_LIMIT),
        name="attn_sample",
    )(q, qlat, clat, ckrt, nlat, nkr)


def _stage_out_body(x_ref, o_ref, u_ref, hist_ref, sg_ref, p_ref, bconv_ref, wconv_ref, lng_ref, lnb_ref,
                    wao_ref, wco_ref, wout_ref, gffn_ref, wup_ref, wdn_ref, gple_ref, wpg_ref, wpp_ref, gfin_ref,
                    *rest, nb, tt, tiles_per_seq, absorb, ff_chunk):
    if absorb:
        wuv_ref, y_ref, uext_ref, shift_ref = rest
    else:
        y_ref, uext_ref, shift_ref = rest
    t = pl.program_id(0) % tiles_per_seq
    x = x_ref[...]
    d = x.shape[-1]
    conv_ch = u_ref.shape[-1]

    @pl.when(t == 0)
    def _():
        uext_ref[:, :HIST_ROWS, :] = hist_ref[...]

    uext_ref[:, HIST_ROWS:, :] = u_ref[...]
    dw = jnp.zeros((nb, tt, conv_ch), F32) + bconv_ref[...][None]
    for r in range(SUBLANES):
        offs = [o for o in range(HIST_PAD, HIST_PAD + CONV_W) if o % SUBLANES == r]
        src_ref = uext_ref
        if r:
            n_rows = tt + max(offs) - r
            shift_ref[:, :n_rows, :] = uext_ref[:, r:r + n_rows, :]
            src_ref = shift_ref
        for o in offs:
            k = o - HIST_PAD
            dw = dw + src_ref[:, o - r:o - r + tt, :] * wconv_ref[k:k + 1, :][None]
    uext_ref[:, :HIST_ROWS, :] = uext_ref[:, tt:tt + HIST_ROWS, :]
    mu = jnp.mean(dw, axis=-1, keepdims=True)
    dc = dw - mu
    ln = dc * lax.rsqrt(jnp.mean(dc * dc, axis=-1, keepdims=True) + LN_EPS)
    ln = ln * lng_ref[...][None] + lnb_ref[...][None]
    cact = (ln * jax.nn.sigmoid(ln)).astype(BF16).reshape(nb * tt, conv_ch)

    if absorb:
        o = _dot(o_ref[...], wuv_ref[...]).astype(BF16)
        a = _dot(o, wao_ref[...])
    else:
        a = lax.dot_general(o_ref[0], wao_ref[...], TN_DIMS, preferred_element_type=F32)
    cb = _dot(cact, wco_ref[...])
    m = sg_ref[:, :d] * a + sg_ref[:, d:] * cb
    x = x + _dot(m.astype(BF16), wout_ref[...])

    h = _rms(x, gffn_ref[...]).astype(BF16)
    d_ff = wup_ref.shape[-1]
    ff = jnp.zeros_like(x)
    for c in range(d_ff // ff_chunk):
        sl = slice(c * ff_chunk, (c + 1) * ff_chunk)
        up = jnp.maximum(_dot(h, wup_ref[:, sl]), 0.0)
        ff = ff + _dot((up * up).astype(BF16), wdn_ref[sl, :])
    x = x + ff

    h = _rms(x, gple_ref[...]).astype(BF16)
    pg = jax.nn.sigmoid(_dot(h, wpg_ref[...]))
    x = x + pg * _dot(p_ref[...].astype(BF16), wpp_ref[...])
    y_ref[...] = _rms(x, gfin_ref[...])


def _stage_out(x, o, u, hist, sg, p, wts, *, nb, tt, absorb):
    n, d = x.shape
    b, t_len, conv_ch = u.shape
    tm = nb * tt
    tiles_per_seq = t_len // tt
    names = ["bconv", "wconv", "lng", "lnb", "wao", "wco", "wout", "gffn", "wup", "wdn", "gple", "wpg", "wpp",
             "gfin"]
    if absorb:
        names.append("wuv_bd")

    def row_spec(a):
        return pl.BlockSpec((tm, a.shape[-1]), lambda i: (i, 0))

    if absorb:
        o_spec = row_spec(o)
    else:
        o_spec = pl.BlockSpec((1, o.shape[1], tm), lambda i: (i // tiles_per_seq, 0, i % tiles_per_seq))
    u_spec = pl.BlockSpec((nb, tt, conv_ch), lambda i: (i // tiles_per_seq, i % tiles_per_seq, 0))
    hist_spec = pl.BlockSpec((nb, HIST_ROWS, conv_ch), lambda i: (i // tiles_per_seq, 0, 0))

    return pl.pallas_call(
        functools.partial(_stage_out_body, nb=nb, tt=tt, tiles_per_seq=tiles_per_seq, absorb=absorb,
                          ff_chunk=1024),
        grid=(n // tm,),
        in_specs=[row_spec(x), o_spec, u_spec, hist_spec, row_spec(sg), row_spec(p)]
        + [_const_spec(wts[k].shape) for k in names],
        out_specs=pl.BlockSpec((tm, d), lambda i: (i, 0)),
        out_shape=jax.ShapeDtypeStruct((n, d), F32),
        scratch_shapes=[pltpu.VMEM((nb, tt + HIST_ROWS, conv_ch), F32),
                        pltpu.VMEM((nb, tt + HIST_ROWS, conv_ch), F32)],
        compiler_params=pltpu.CompilerParams(dimension_semantics=("arbitrary",),
                                             vmem_limit_bytes=VMEM_LIMIT),
        name="stage_out_absorb" if absorb else "stage_out",
    )(x, o, u, hist, sg, p, *[wts[k] for k in names])


def _head_slabs(nope, rope):
    c, hds = (nope if nope is not None else rope).shape[:2]
    parts = [nope if nope is not None else jnp.zeros((c, hds, QK_NOPE), F32),
             rope if rope is not None else jnp.zeros((c, hds, QK_ROPE), F32),
             jnp.zeros((c, hds, HEAD_SLAB - QK_NOPE - QK_ROPE), F32)]
    return jnp.concatenate(parts, axis=-1).reshape(c, hds * HEAD_SLAB)


def _prep_weights(norm_mix_g, w_in, q_norm_g, w_uq, kv_norm_g, w_uk, w_uv, w_attn_out, conv_w, conv_b,
                  conv_ln_g, conv_ln_b, w_conv_out, w_out, norm_ffn_g, w_ff_up, w_ff_down, ple_norm_g,
                  w_ple_gate, w_ple_proj, final_norm_g):
    d = w_in.shape[0]
    c_q = q_norm_g.shape[-1]
    c_kv = kv_norm_g.shape[-1]
    conv_ch = conv_b.shape[-1]
    o_kv = c_q
    o_kr = o_kv + c_kv
    o_conv = o_kr + QK_ROPE
    o_gate = o_conv + 2 * conv_ch
    w_kr = w_in[:, o_kr:o_conv]
    zn = jnp.zeros((d, QK_NOPE), F32)
    zp = jnp.zeros((d, HEAD_SLAB - QK_NOPE - QK_ROPE), F32)
    wa = jnp.concatenate([w_in[:, :o_kr], zn, w_kr, zp, w_in[:, o_conv:o_gate]], axis=1)
    q_nope, q_rope = w_uq[..., :QK_NOPE], w_uq[..., QK_NOPE:]
    row = lambda v: v.reshape(1, -1).astype(F32)
    wukt = jnp.transpose(w_uk, (1, 2, 0))
    wukt = jnp.concatenate([wukt, jnp.zeros((N_HEADS, HEAD_SLAB - QK_NOPE, c_kv), F32)], axis=1)
    eye = jnp.eye(N_HEADS, dtype=F32)
    wuv_bd = jnp.einsum("chd,hg->hcgd", w_uv, eye).reshape(N_HEADS * c_kv, N_HEADS * V_DIM)
    return {
        "gmix": row(norm_mix_g), "wa": wa.astype(BF16), "gq": row(q_norm_g), "gkv": row(kv_norm_g),
        "wq": _head_slabs(q_nope, q_rope).astype(BF16),
        "wkn": _head_slabs(w_uk, None).astype(BF16),
        "wv": w_uv.reshape(c_kv, N_HEADS * V_DIM).T.astype(BF16),
        "wukt": wukt.astype(BF16), "wuv_bd": wuv_bd.astype(BF16),
        "bconv": row(conv_b), "wconv": conv_w.astype(F32), "lng": row(conv_ln_g), "lnb": row(conv_ln_b),
        "wg": w_in[:, o_gate:].astype(BF16), "wao": w_attn_out.astype(BF16),
        "wco": w_conv_out.astype(BF16), "wout": w_out.astype(BF16), "gffn": row(norm_ffn_g),
        "wup": w_ff_up.astype(BF16), "wdn": w_ff_down.astype(BF16), "gple": row(ple_norm_g),
        "wpg": w_ple_gate.astype(BF16), "wpp": w_ple_proj.astype(BF16), "gfin": row(final_norm_g),
    }


def _rope_tables(pos, q_scale):
    half = QK_ROPE // 2
    inv = ROPE_THETA ** (-jnp.arange(half, dtype=F32) / half)
    ang = pos.astype(F32)[:, None] * inv[None, :]
    cos, sin = jnp.cos(ang), jnp.sin(ang)
    n = pos.shape[0]
    pad = jnp.zeros((n, HEAD_SLAB - QK_NOPE - QK_ROPE), F32)
    cos_t = jnp.concatenate([jnp.ones((n, QK_NOPE), F32), cos, cos, pad], axis=1)
    sin_t = jnp.concatenate([jnp.zeros((n, QK_NOPE), F32), -sin, sin, pad], axis=1)
    return jnp.concatenate([cos_t * q_scale, sin_t * q_scale, cos_t, sin_t], axis=1)


def _conv_cache(hist, u):
    n_hist = CONV_W - 1
    t_len = u.shape[1]
    if t_len >= n_hist:
        return u[:, t_len - n_hist:]
    return jnp.concatenate([hist[:, HIST_ROWS - (n_hist - t_len):], u], axis=1)


def _tile(n, pref):
    return pref if n % pref == 0 else n


def kernel(x_prompt, x_sample, p_prompt, p_sample, cache_kv_latent, cache_k_rope, cache_conv, norm_mix_g, w_in, q_norm_g, w_uq, kv_norm_g, w_uk, w_uv, w_attn_out, conv_w, conv_b, conv_ln_g, conv_ln_b, w_conv_out, w_out, norm_ffn_g, w_ff_up, w_ff_down, ple_norm_g, w_ple_gate, w_ple_proj, final_norm_g):
    depth = w_in.shape[0]
    assert depth == 1, "one layer: the two request groups are independent within it"
    bp, tp, d = x_prompt.shape
    bs, ts, _ = x_sample.shape
    past = cache_kv_latent.shape[2]
    conv_ch = conv_b.shape[-1]
    wts = _prep_weights(norm_mix_g[0], w_in[0], q_norm_g[0], w_uq[0], kv_norm_g[0], w_uk[0], w_uv[0],
                        w_attn_out[0], conv_w[0], conv_b[0], conv_ln_g[0], conv_ln_b[0], w_conv_out[0],
                        w_out[0], norm_ffn_g[0], w_ff_up[0], w_ff_down[0], ple_norm_g[0], w_ple_gate[0],
                        w_ple_proj[0], final_norm_g)

    tt = _tile(tp, 512)
    hist0 = jnp.zeros((bp, HIST_ROWS, conv_ch), F32)
    q, k, vt, lat_p, kr_p, u_p, sg_p = _stage_in(x_prompt, _rope_tables(jnp.arange(tp), EXP2_SCALE), wts,
                                                  nb=1, tt=tt, absorb=False)
    o_p = _attn_prompt(q, k, vt, tq=ATTN_TQ, qc=ATTN_QC)
    n_p = bp * tp
    y_p = _stage_out(x_prompt.reshape(n_p, d), o_p, u_p, hist0, sg_p.reshape(n_p, -1),
                     p_prompt[0].reshape(n_p, -1), wts, nb=1, tt=tt, absorb=False)

    hist_s = jnp.pad(cache_conv[0], ((0, 0), (HIST_PAD, 0), (0, 0)))
    qs, qlat, lat_s, kr_s, u_s, sg_s = _stage_in(x_sample, _rope_tables(past + jnp.arange(ts), 1.0), wts,
                                                 nb=bs, tt=ts, absorb=True)
    o_s = _attn_sample(qs, qlat, cache_kv_latent[0], jnp.swapaxes(cache_k_rope[0], 1, 2), lat_s, kr_s)
    n_s = bs * ts
    y_s = _stage_out(x_sample.reshape(n_s, d), o_s.reshape(n_s, -1), u_s, hist_s, sg_s.reshape(n_s, -1),
                     p_sample[0].reshape(n_s, -1), wts, nb=bs, tt=ts, absorb=True)

    ncv_p = _conv_cache(hist0, u_p)
    ncv_s = _conv_cache(hist_s, u_s)

    return (y_p.reshape(bp, tp, d), y_s.reshape(bs, ts, d), lat_p[None], kr_p[None], ncv_p[None],
            lat_s[None], kr_s[None], ncv_s[None])
```

```python
import functools
import math

import jax
import jax.numpy as jnp
from jax import lax
from jax.experimental import pallas as pl
from jax.experimental.pallas import tpu as pltpu

F32 = jnp.float32
BF16 = jnp.bfloat16

CHUNK = 64
N_HEADS = 8
QK_NOPE = 64
QK_ROPE = 32
V_DIM = 64
CONV_W = 31
ROPE_THETA = 10000.0
RMS_EPS = 1e-6
LN_EPS = 1e-5
SM_SCALE = (QK_NOPE + QK_ROPE) ** -0.5
EXP2_SCALE = SM_SCALE * math.log2(math.e)
NEG_INF = -1e30

LANES = 128
SUBLANES = 8
HEAD_SLAB = LANES
HIST_ROWS = 32
HIST_PAD = HIST_ROWS - (CONV_W - 1)
ATTN_TQ = 1024
ATTN_QC = 256
SAMPLE_BB = 2
BF16_SUBLANES = 16
ACC_ROWS = V_DIM + BF16_SUBLANES
VMEM_LIMIT = 56 * 1024 * 1024


def _rms(x, g):
    return x * lax.rsqrt(jnp.mean(x * x, axis=-1, keepdims=True) + RMS_EPS) * g


def _dot(a, b):
    return jnp.dot(a, b, preferred_element_type=F32)


NT_DIMS = (((1,), (1,)), ((), ()))
TN_DIMS = (((0,), (0,)), ((), ()))


def _const_spec(shape):
    nd = len(shape)
    return pl.BlockSpec(shape, lambda *_: (0,) * nd, pipeline_mode=pl.Buffered(1))


def _stage_in_body(x_ref, cs_ref, gmix_ref, wa_ref, wg_ref, gq_ref, gkv_ref, wq_ref,
                   *rest, nb, tt, absorb, c_q, c_kv, conv_ch):
    if absorb:
        (wukt_ref,) = rest[:1]
        q_ref, qlat_ref, lat_ref, kr_ref, u_ref, sg_ref = rest[1:]
    else:
        wkn_ref, wv_ref = rest[:2]
        q_ref, k_ref, v_ref, lat_ref, kr_ref, u_ref, sg_ref = rest[2:]

    rows = nb * tt
    d = x_ref.shape[-1]

    x = x_ref[...].reshape(rows, d)
    h = _rms(x, gmix_ref[...]).astype(BF16)
    z = _dot(h, wa_ref[...])
    o_kv = c_q
    o_ks = o_kv + c_kv
    o_u = o_ks + HEAD_SLAB

    sg_ref[...] = jax.nn.sigmoid(_dot(h, wg_ref[...])).astype(BF16).reshape(nb, tt, sg_ref.shape[-1])

    cos_q, sin_q, cos_k, sin_k = (cs_ref[:, i * HEAD_SLAB:(i + 1) * HEAD_SLAB] for i in range(4))
    half = QK_ROPE // 2
    first_half = lax.broadcasted_iota(jnp.int32, (rows, HEAD_SLAB), 1) < QK_NOPE + half

    def rope3(a, cos, sin):
        swapped = jnp.where(first_half, pltpu.roll(a, HEAD_SLAB - half, axis=1), pltpu.roll(a, half, axis=1))
        a3 = a.reshape(nb, tt, HEAD_SLAB) * cos[None] + swapped.reshape(nb, tt, HEAD_SLAB) * sin[None]
        return a3.reshape(rows, HEAD_SLAB)

    cqn = _rms(z[:, :c_q], gq_ref[...]).astype(BF16)
    qa = _dot(cqn, wq_ref[...])
    q_heads = []
    for hd in range(N_HEADS):
        sl = slice(hd * HEAD_SLAB, (hd + 1) * HEAD_SLAB)
        qh = rope3(qa[:, sl], cos_q, sin_q).astype(BF16)
        q_heads.append(qh)
        q_ref[:, :, sl] = qh.reshape(nb, tt, HEAD_SLAB)

    latent = _rms(z[:, o_kv:o_ks], gkv_ref[...])
    lat_ref[...] = latent.reshape(nb, tt, c_kv)
    kslab = rope3(z[:, o_ks:o_u], cos_k, sin_k)
    if absorb:
        kr_ref[...] = kslab[:, QK_NOPE:QK_NOPE + QK_ROPE].reshape(nb, tt, QK_ROPE)
    else:
        kr_ref[0] = kslab.T[QK_NOPE:QK_NOPE + QK_ROPE, :]

    if absorb:
        for hd in range(N_HEADS):
            ql = _dot(q_heads[hd], wukt_ref[hd])
            qlat_ref[:, :, hd * c_kv:(hd + 1) * c_kv] = ql.astype(BF16).reshape(nb, tt, c_kv)
    else:
        lat_b = latent.astype(BF16)
        kn = _dot(lat_b, wkn_ref[...])
        for hd in range(N_HEADS):
            sl = slice(hd * HEAD_SLAB, (hd + 1) * HEAD_SLAB)
            k_ref[:, :, sl] = (kn[:, sl] + kslab).astype(BF16).reshape(nb, tt, HEAD_SLAB)
        v_ref[0] = lax.dot_general(wv_ref[...], lat_b, NT_DIMS, preferred_element_type=F32).astype(BF16)

    u = z[:, o_u:o_u + conv_ch] * jax.nn.sigmoid(z[:, o_u + conv_ch:o_u + 2 * conv_ch])
    u_ref[...] = u.reshape(nb, tt, conv_ch)


def _stage_in(x, cs, wts, *, nb, tt, absorb):
    b, t_len, d = x.shape
    c_q = wts["gq"].shape[-1]
    c_kv = wts["gkv"].shape[-1]
    conv_ch = wts["bconv"].shape[-1]
    n_gate = wts["wg"].shape[-1]
    grid = (b // nb, t_len // tt)

    def row_spec(width):
        return pl.BlockSpec((nb, tt, width), lambda i, j: (i, j, 0))

    in_specs = [row_spec(d), pl.BlockSpec((tt, 4 * HEAD_SLAB), lambda i, j: (j, 0))]
    names = ["gmix", "wa", "wg", "gq", "gkv", "wq"]
    names += ["wukt"] if absorb else ["wkn", "wv"]
    in_specs += [_const_spec(wts[n].shape) for n in names]

    qk_w = N_HEADS * HEAD_SLAB
    out_shapes = [jax.ShapeDtypeStruct((b, t_len, qk_w), BF16)]
    out_specs = [row_spec(qk_w)]
    if absorb:
        out_shapes.append(jax.ShapeDtypeStruct((b, t_len, N_HEADS * c_kv), BF16))
        out_specs.append(row_spec(N_HEADS * c_kv))
    else:
        assert nb == 1, "transposed value tiles are written one sequence at a time"
        out_shapes += [jax.ShapeDtypeStruct((b, t_len, qk_w), BF16),
                       jax.ShapeDtypeStruct((b, N_HEADS * V_DIM, t_len), BF16)]
        out_specs += [row_spec(qk_w), pl.BlockSpec((1, N_HEADS * V_DIM, tt), lambda i, j: (i, 0, j))]
    if absorb:
        kr_shape, kr_spec = jax.ShapeDtypeStruct((b, t_len, QK_ROPE), F32), row_spec(QK_ROPE)
    else:
        kr_shape = jax.ShapeDtypeStruct((b, QK_ROPE, t_len), F32)
        kr_spec = pl.BlockSpec((1, QK_ROPE, tt), lambda i, j: (i, 0, j))
    out_shapes += [jax.ShapeDtypeStruct((b, t_len, c_kv), F32), kr_shape,
                   jax.ShapeDtypeStruct((b, t_len, conv_ch), F32),
                   jax.ShapeDtypeStruct((b, t_len, n_gate), BF16)]
    out_specs += [row_spec(c_kv), kr_spec, row_spec(conv_ch), row_spec(n_gate)]

    body = functools.partial(_stage_in_body, nb=nb, tt=tt, absorb=absorb, c_q=c_q, c_kv=c_kv,
                             conv_ch=conv_ch)
    return pl.pallas_call(
        body,
        grid=grid,
        in_specs=in_specs,
        out_specs=out_specs,
        out_shape=out_shapes,
        compiler_params=pltpu.CompilerParams(dimension_semantics=("arbitrary", "arbitrary"),
                                             vmem_limit_bytes=VMEM_LIMIT),
        name="stage_in_absorb" if absorb else "stage_in",
    )(x, cs, *[wts[n] for n in names])


def _attn_prompt_body(q_ref, k_ref, vt_ref, o_ref, sa_ref, sb_ref, m_ref, acc_ref, *, tq, qc, n_q):
    n_col = tq // qc
    slab = lambda hh: slice(hh * HEAD_SLAB, (hh + 1) * HEAD_SLAB)
    ones_rows = jnp.ones((ACC_ROWS - V_DIM, tq), BF16)

    def scores_into(kt, s_buf, diag):
        for hh in range(2):
            if not diag:
                k = k_ref[0, kt * tq:(kt + 1) * tq, slab(hh)]
                s_buf[hh] = lax.dot_general(k, q_ref[0, :, slab(hh)], NT_DIMS, preferred_element_type=F32)
                continue
            for cb in range(n_col):
                n_keys = (cb + 1) * qc
                k = k_ref[0, kt * tq:kt * tq + n_keys, slab(hh)]
                s_buf[hh, :n_keys, cb * qc:(cb + 1) * qc] = lax.dot_general(
                    k, q_ref[0, cb * qc:(cb + 1) * qc, slab(hh)], NT_DIMS, preferred_element_type=F32)

    def update(c, s, vt, mask):
        if mask is not None:
            s = jnp.where(mask, s, NEG_INF)
        m = m_ref[c]
        m_new = jnp.maximum(m, jnp.max(s, axis=0, keepdims=True))
        p = jnp.exp2(s - m_new).astype(BF16)
        vt_aug = jnp.concatenate([vt, ones_rows[:, :vt.shape[1]]], axis=0)
        acc_ref[c] = jnp.exp2(m - m_new) * acc_ref[c] + _dot(vt_aug, p)
        m_ref[c] = m_new

    def full_step(kt, s_cur, s_nxt, next_is_diag):
        scores_into(kt + 1, s_nxt, next_is_diag)
        for hh in range(2):
            vt = vt_ref[0, hh * V_DIM:(hh + 1) * V_DIM, kt * tq:(kt + 1) * tq]
            for cb in range(n_col):
                update(hh * n_col + cb, s_cur[hh, :, cb * qc:(cb + 1) * qc], vt, None)

    def diag_step(qt, s_cur):
        for hh in range(2):
            for cb in range(n_col):
                c = hh * n_col + cb
                n_keys = (cb + 1) * qc
                vt = vt_ref[0, hh * V_DIM:(hh + 1) * V_DIM, qt * tq:qt * tq + n_keys]
                key_chunk = lax.broadcasted_iota(jnp.int32, (n_keys, qc), 0) // CHUNK
                qry_chunk = (lax.broadcasted_iota(jnp.int32, (n_keys, qc), 1) + cb * qc) // CHUNK
                update(c, s_cur[hh, :n_keys, cb * qc:(cb + 1) * qc], vt, key_chunk <= qry_chunk)
                acc = acc_ref[c]
                o_ref[0, hh * V_DIM:(hh + 1) * V_DIM, cb * qc:(cb + 1) * qc] = (
                    acc[:V_DIM] / acc[V_DIM:V_DIM + 1]).astype(BF16)

    bufs = (sa_ref, sb_ref)
    for qt in range(n_q):
        @pl.when(pl.program_id(2) == qt)
        def _(qt=qt):
            m_ref[...] = jnp.full(m_ref.shape, NEG_INF, F32)
            acc_ref[...] = jnp.zeros(acc_ref.shape, F32)
            scores_into(0, bufs[0], qt == 0)
            for kt in range(qt):
                full_step(kt, bufs[kt % 2], bufs[(kt + 1) % 2], kt + 1 == qt)
            diag_step(qt, bufs[qt % 2])


def _attn_prompt(q, k, vt, *, tq, qc):
    b, t_len, _ = q.shape
    grid = (b, N_HEADS // 2, t_len // tq)
    n_chains = 2 * (tq // qc)
    return pl.pallas_call(
        functools.partial(_attn_prompt_body, tq=tq, qc=qc, n_q=t_len // tq),
        grid=grid,
        in_specs=[pl.BlockSpec((1, tq, 2 * HEAD_SLAB), lambda i, p, j: (i, j, p)),
                  pl.BlockSpec((1, t_len, 2 * HEAD_SLAB), lambda i, p, j: (i, 0, p)),
                  pl.BlockSpec((1, 2 * V_DIM, t_len), lambda i, p, j: (i, p, 0))],
        out_specs=pl.BlockSpec((1, 2 * V_DIM, tq), lambda i, p, j: (i, p, j)),
        out_shape=jax.ShapeDtypeStruct((b, N_HEADS * V_DIM, t_len), BF16),
        scratch_shapes=[pltpu.VMEM((2, tq, tq), F32), pltpu.VMEM((2, tq, tq), F32),
                        pltpu.VMEM((n_chains, 1, qc), F32), pltpu.VMEM((n_chains, ACC_ROWS, qc), F32)],
        compiler_params=pltpu.CompilerParams(dimension_semantics=("arbitrary",) * 3,
                                             vmem_limit_bytes=VMEM_LIMIT),
        name="attn_prompt",
    )(q, k, vt)


def _attn_sample_body(q_ref, qlat_ref, clat_ref, ckrt_ref, nlat_ref, nkr_ref, o_ref, *, past, tt, c_kv):
    for i in range(q_ref.shape[0]):
        _attn_sample_one(i, q_ref, qlat_ref, clat_ref, ckrt_ref, nlat_ref, nkr_ref, o_ref, past, tt, c_kv)


def _attn_sample_one(i, q_ref, qlat_ref, clat_ref, ckrt_ref, nlat_ref, nkr_ref, o_ref, past, tt, c_kv):
    ql = jnp.concatenate([qlat_ref[i, :, hd * c_kv:(hd + 1) * c_kv] for hd in range(N_HEADS)], axis=0)
    qs = jnp.concatenate([q_ref[i, :, hd * HEAD_SLAB:(hd + 1) * HEAD_SLAB] for hd in range(N_HEADS)],
                         axis=0)
    qr = qs[:, QK_NOPE:QK_NOPE + QK_ROPE]

    clat = clat_ref[i].astype(BF16)
    ckrt = ckrt_ref[i].astype(BF16)
    nlat = nlat_ref[i].astype(BF16)
    nkr = nkr_ref[i].astype(BF16)
    s_c = (lax.dot_general(ql, clat, NT_DIMS, preferred_element_type=F32)
           + _dot(qr, ckrt)) * SM_SCALE
    s_n = (lax.dot_general(ql, nlat, NT_DIMS, preferred_element_type=F32)
           + lax.dot_general(qr, nkr, NT_DIMS, preferred_element_type=F32)) * SM_SCALE
    rows = N_HEADS * tt
    q_pos = past + lax.broadcasted_iota(jnp.int32, (rows, tt), 0) % tt
    k_pos = past + lax.broadcasted_iota(jnp.int32, (rows, tt), 1)
    s_n = jnp.where(k_pos // CHUNK <= q_pos // CHUNK, s_n, NEG_INF)

    m = jnp.maximum(jnp.max(s_c, axis=-1, keepdims=True), jnp.max(s_n, axis=-1, keepdims=True))
    p_c = jnp.exp(s_c - m)
    p_n = jnp.exp(s_n - m)
    l = jnp.sum(p_c, axis=-1, keepdims=True) + jnp.sum(p_n, axis=-1, keepdims=True)
    o = (_dot(p_c.astype(BF16), clat) + _dot(p_n.astype(BF16), nlat)) / l
    for hd in range(N_HEADS):
        o_ref[i, :, hd * c_kv:(hd + 1) * c_kv] = o[hd * tt:(hd + 1) * tt, :].astype(BF16)


def _attn_sample(q, qlat, clat, ckrt, nlat, nkr):
    b, tt, _ = q.shape
    past, c_kv = clat.shape[1], clat.shape[2]
    bb = SAMPLE_BB if b % SAMPLE_BB == 0 else 1

    def spec(a):
        return pl.BlockSpec((bb,) + a.shape[1:], lambda i: (i, 0, 0))

    return pl.pallas_call(
        functools.partial(_attn_sample_body, past=past, tt=tt, c_kv=c_kv),
        grid=(b // bb,),
        in_specs=[spec(a) for a in (q, qlat, clat, ckrt, nlat, nkr)],
        out_specs=pl.BlockSpec((bb, tt, N_HEADS * c_kv), lambda i: (i, 0, 0)),
        out_shape=jax.ShapeDtypeStruct((b, tt, N_HEADS * c_kv), BF16),
        compiler_params=pltpu.CompilerParams(dimension_semantics=("arbitrary",),
                                             vmem_limit_bytes=VMEM_LIMIT),
        name="attn_sample",
    )(q, qlat, clat, ckrt, nlat, nkr)


def _stage_out_body(x_ref, o_ref, u_ref, hist_ref, sg_ref, p_ref, bconv_ref, wconv_ref, lng_ref, lnb_ref,
                    wao_ref, wco_ref, wout_ref, gffn_ref, wup_ref, wdn_ref, gple_ref, wpg_ref, wpp_ref, gfin_ref,
                    *rest, nb, tt, tiles_per_seq, absorb, ff_chunk):
    if absorb:
        wuv_ref, y_ref, uext_ref, shift_ref = rest
    else:
        y_ref, uext_ref, shift_ref = rest
    t = pl.program_id(0) % tiles_per_seq
    x = x_ref[...]
    d = x.shape[-1]
    conv_ch = u_ref.shape[-1]

    @pl.when(t == 0)
    def _():
        uext_ref[:, :HIST_ROWS, :] = hist_ref[...]

    uext_ref[:, HIST_ROWS:, :] = u_ref[...]
    dw = jnp.zeros((nb, tt, conv_ch), F32) + bconv_ref[...][None]
    for r in range(SUBLANES):
        offs = [o for o in range(HIST_PAD, HIST_PAD + CONV_W) if o % SUBLANES == r]
        src_ref = uext_ref
        if r:
            n_rows = tt + max(offs) - r
            shift_ref[:, :n_rows, :] = uext_ref[:, r:r + n_rows, :]
            src_ref = shift_ref
        for o in offs:
            k = o - HIST_PAD
            dw = dw + src_ref[:, o - r:o - r + tt, :] * wconv_ref[k:k + 1, :][None]
    uext_ref[:, :HIST_ROWS, :] = uext_ref[:, tt:tt + HIST_ROWS, :]
    mu = jnp.mean(dw, axis=-1, keepdims=True)
    dc = dw - mu
    ln = dc * lax.rsqrt(jnp.mean(dc * dc, axis=-1, keepdims=True) + LN_EPS)
    ln = ln * lng_ref[...][None] + lnb_ref[...][None]
    cact = (ln * jax.nn.sigmoid(ln)).astype(BF16).reshape(nb * tt, conv_ch)

    if absorb:
        o = _dot(o_ref[...], wuv_ref[...]).astype(BF16)
        a = _dot(o, wao_ref[...])
    else:
        a = lax.dot_general(o_ref[0], wao_ref[...], TN_DIMS, preferred_element_type=F32)
    cb = _dot(cact, wco_ref[...])
    m = sg_ref[:, :d] * a + sg_ref[:, d:] * cb
    x = x + _dot(m.astype(BF16), wout_ref[...])

    h = _rms(x, gffn_ref[...]).astype(BF16)
    d_ff = wup_ref.shape[-1]
    ff = jnp.zeros_like(x)
    for c in range(d_ff // ff_chunk):
        sl = slice(c * ff_chunk, (c + 1) * ff_chunk)
        up = jnp.maximum(_dot(h, wup_ref[:, sl]), 0.0)
        ff = ff + _dot((up * up).astype(BF16), wdn_ref[sl, :])
    x = x + ff

    h = _rms(x, gple_ref[...]).astype(BF16)
    pg = jax.nn.sigmoid(_dot(h, wpg_ref[...]))
    x = x + pg * _dot(p_ref[...].astype(BF16), wpp_ref[...])
    y_ref[...] = _rms(x, gfin_ref[...])


def _stage_out(x, o, u, hist, sg, p, wts, *, nb, tt, absorb):
    n, d = x.shape
    b, t_len, conv_ch = u.shape
    tm = nb * tt
    tiles_per_seq = t_len // tt
    names = ["bconv", "wconv", "lng", "lnb", "wao", "wco", "wout", "gffn", "wup", "wdn", "gple", "wpg", "wpp",
             "gfin"]
    if absorb:
        names.append("wuv_bd")

    def row_spec(a):
        return pl.BlockSpec((tm, a.shape[-1]), lambda i: (i, 0))

    if absorb:
        o_spec = row_spec(o)
    else:
        o_spec = pl.BlockSpec((1, o.shape[1], tm), lambda i: (i // tiles_per_seq, 0, i % tiles_per_seq))
    u_spec = pl.BlockSpec((nb, tt, conv_ch), lambda i: (i // tiles_per_seq, i % tiles_per_seq, 0))
    hist_spec = pl.BlockSpec((nb, HIST_ROWS, conv_ch), lambda i: (i // tiles_per_seq, 0, 0))

    return pl.pallas_call(
        functools.partial(_stage_out_body, nb=nb, tt=tt, tiles_per_seq=tiles_per_seq, absorb=absorb,
                          ff_chunk=1024),
        grid=(n // tm,),
        in_specs=[row_spec(x), o_spec, u_spec, hist_spec, row_spec(sg), row_spec(p)]
        + [_const_spec(wts[k].shape) for k in names],
        out_specs=pl.BlockSpec((tm, d), lambda i: (i, 0)),
        out_shape=jax.ShapeDtypeStruct((n, d), F32),
        scratch_shapes=[pltpu.VMEM((nb, tt + HIST_ROWS, conv_ch), F32),
                        pltpu.VMEM((nb, tt + HIST_ROWS, conv_ch), F32)],
        compiler_params=pltpu.CompilerParams(dimension_semantics=("arbitrary",),
                                             vmem_limit_bytes=VMEM_LIMIT),
        name="stage_out_absorb" if absorb else "stage_out",
    )(x, o, u, hist, sg, p, *[wts[k] for k in names])


def _head_slabs(nope, rope):
    c, hds = (nope if nope is not None else rope).shape[:2]
    parts = [nope if nope is not None else jnp.zeros((c, hds, QK_NOPE), F32),
             rope if rope is not None else jnp.zeros((c, hds, QK_ROPE), F32),
             jnp.zeros((c, hds, HEAD_SLAB - QK_NOPE - QK_ROPE), F32)]
    return jnp.concatenate(parts, axis=-1).reshape(c, hds * HEAD_SLAB)


def _prep_weights(norm_mix_g, w_in, q_norm_g, w_uq, kv_norm_g, w_uk, w_uv, w_attn_out, conv_w, conv_b,
                  conv_ln_g, conv_ln_b, w_conv_out, w_out, norm_ffn_g, w_ff_up, w_ff_down, ple_norm_g,
                  w_ple_gate, w_ple_proj, final_norm_g):
    d = w_in.shape[0]
    c_q = q_norm_g.shape[-1]
    c_kv = kv_norm_g.shape[-1]
    conv_ch = conv_b.shape[-1]
    o_kv = c_q
    o_kr = o_kv + c_kv
    o_conv = o_kr + QK_ROPE
    o_gate = o_conv + 2 * conv_ch
    w_kr = w_in[:, o_kr:o_conv]
    zn = jnp.zeros((d, QK_NOPE), F32)
    zp = jnp.zeros((d, HEAD_SLAB - QK_NOPE - QK_ROPE), F32)
    wa = jnp.concatenate([w_in[:, :o_kr], zn, w_kr, zp, w_in[:, o_conv:o_gate]], axis=1)
    q_nope, q_rope = w_uq[..., :QK_NOPE], w_uq[..., QK_NOPE:]
    row = lambda v: v.reshape(1, -1).astype(F32)
    wukt = jnp.transpose(w_uk, (1, 2, 0))
    wukt = jnp.concatenate([wukt, jnp.zeros((N_HEADS, HEAD_SLAB - QK_NOPE, c_kv), F32)], axis=1)
    eye = jnp.eye(N_HEADS, dtype=F32)
    wuv_bd = jnp.einsum("chd,hg->hcgd", w_uv, eye).reshape(N_HEADS * c_kv, N_HEADS * V_DIM)
    return {
        "gmix": row(norm_mix_g), "wa": wa.astype(BF16), "gq": row(q_norm_g), "gkv": row(kv_norm_g),
        "wq": _head_slabs(q_nope, q_rope).astype(BF16),
        "wkn": _head_slabs(w_uk, None).astype(BF16),
        "wv": w_uv.reshape(c_kv, N_HEADS * V_DIM).T.astype(BF16),
        "wukt": wukt.astype(BF16), "wuv_bd": wuv_bd.astype(BF16),
        "bconv": row(conv_b), "wconv": conv_w.astype(F32), "lng": row(conv_ln_g), "lnb": row(conv_ln_b),
        "wg": w_in[:, o_gate:].astype(BF16), "wao": w_attn_out.astype(BF16),
        "wco": w_conv_out.astype(BF16), "wout": w_out.astype(BF16), "gffn": row(norm_ffn_g),
        "wup": w_ff_up.astype(BF16), "wdn": w_ff_down.astype(BF16), "gple": row(ple_norm_g),
        "wpg": w_ple_gate.astype(BF16), "wpp": w_ple_proj.astype(BF16), "gfin": row(final_norm_g),
    }


def _rope_tables(pos, q_scale):
    half = QK_ROPE // 2
    inv = ROPE_THETA ** (-jnp.arange(half, dtype=F32) / half)
    ang = pos.astype(F32)[:, None] * inv[None, :]
    cos, sin = jnp.cos(ang), jnp.sin(ang)
    lanes = ((0, 0), (QK_NOPE, HEAD_SLAB - QK_NOPE - QK_ROPE))
    cos_t = jnp.pad(jnp.concatenate([cos, cos], axis=1), lanes)
    cos_t = cos_t + (jnp.arange(HEAD_SLAB) < QK_NOPE).astype(F32)[None, :]
    sin_t = jnp.pad(jnp.concatenate([-sin, sin], axis=1), lanes)
    return jnp.concatenate([cos_t * q_scale, sin_t * q_scale, cos_t, sin_t], axis=1)


def _conv_cache(hist, u):
    n_hist = CONV_W - 1
    t_len = u.shape[1]
    if t_len >= n_hist:
        return u[:, t_len - n_hist:]
    return jnp.concatenate([hist[:, HIST_ROWS - (n_hist - t_len):], u], axis=1)


def _tile(n, pref):
    return pref if n % pref == 0 else n


def kernel(x_prompt, x_sample, p_prompt, p_sample, cache_kv_latent, cache_k_rope, cache_conv, norm_mix_g, w_in, q_norm_g, w_uq, kv_norm_g, w_uk, w_uv, w_attn_out, conv_w, conv_b, conv_ln_g, conv_ln_b, w_conv_out, w_out, norm_ffn_g, w_ff_up, w_ff_down, ple_norm_g, w_ple_gate, w_ple_proj, final_norm_g):
    depth = w_in.shape[0]
    assert depth == 1, "one layer: the two request groups are independent within it"
    bp, tp, d = x_prompt.shape
    bs, ts, _ = x_sample.shape
    past = cache_kv_latent.shape[2]
    conv_ch = conv_b.shape[-1]
    wts = _prep_weights(norm_mix_g[0], w_in[0], q_norm_g[0], w_uq[0], kv_norm_g[0], w_uk[0], w_uv[0],
                        w_attn_out[0], conv_w[0], conv_b[0], conv_ln_g[0], conv_ln_b[0], w_conv_out[0],
                        w_out[0], norm_ffn_g[0], w_ff_up[0], w_ff_down[0], ple_norm_g[0], w_ple_gate[0],
                        w_ple_proj[0], final_norm_g)

    tt = _tile(tp, 512)
    hist0 = jnp.zeros((bp, HIST_ROWS, conv_ch), F32)
    q, k, vt, lat_p, krt_p, u_p, sg_p = _stage_in(x_prompt, _rope_tables(jnp.arange(tp), EXP2_SCALE), wts,
                                                  nb=1, tt=tt, absorb=False)
    o_p = _attn_prompt(q, k, vt, tq=ATTN_TQ, qc=ATTN_QC)
    n_p = bp * tp
    y_p = _stage_out(x_prompt.reshape(n_p, d), o_p, u_p, hist0, sg_p.reshape(n_p, -1),
                     p_prompt[0].reshape(n_p, -1), wts, nb=1, tt=tt, absorb=False)

    hist_s = jnp.pad(cache_conv[0], ((0, 0), (HIST_PAD, 0), (0, 0)))
    qs, qlat, lat_s, kr_s, u_s, sg_s = _stage_in(x_sample, _rope_tables(past + jnp.arange(ts), 1.0), wts,
                                                 nb=bs, tt=ts, absorb=True)
    o_s = _attn_sample(qs, qlat, cache_kv_latent[0], jnp.swapaxes(cache_k_rope[0], 1, 2), lat_s, kr_s)
    n_s = bs * ts
    y_s = _stage_out(x_sample.reshape(n_s, d), o_s.reshape(n_s, -1), u_s, hist_s, sg_s.reshape(n_s, -1),
                     p_sample[0].reshape(n_s, -1), wts, nb=bs, tt=ts, absorb=True)

    ncv_p = _conv_cache(hist0, u_p)
    ncv_s = _conv_cache(hist_s, u_s)

    kr_p = jnp.swapaxes(krt_p, 1, 2)
    return (y_p.reshape(bp, tp, d), y_s.reshape(bs, ts, d), lat_p[None], kr_p[None], ncv_p[None],
            lat_s[None], kr_s[None], ncv_s[None])
```

```python
import functools
import math

import jax
import jax.numpy as jnp
from jax import lax
from jax.experimental import pallas as pl
from jax.experimental.pallas import tpu as pltpu

F32 = jnp.float32
BF16 = jnp.bfloat16

CHUNK = 64
N_HEADS = 8
QK_NOPE = 64
QK_ROPE = 32
V_DIM = 64
CONV_W = 31
ROPE_THETA = 10000.0
RMS_EPS = 1e-6
LN_EPS = 1e-5
SM_SCALE = (QK_NOPE + QK_ROPE) ** -0.5
EXP2_SCALE = SM_SCALE * math.log2(math.e)
NEG_INF = -1e30

LANES = 128
SUBLANES = 8
HEAD_SLAB = LANES
HIST_ROWS = 32
HIST_PAD = HIST_ROWS - (CONV_W - 1)
ATTN_TQ = 1024
ATTN_QC = 256
SAMPLE_BB = 2
BF16_SUBLANES = 16
ACC_ROWS = V_DIM + BF16_SUBLANES
VMEM_LIMIT = 56 * 1024 * 1024


def _rms(x, g):
    return x * lax.rsqrt(jnp.mean(x * x, axis=-1, keepdims=True) + RMS_EPS) * g


def _dot(a, b):
    return jnp.dot(a, b, preferred_element_type=F32)


NT_DIMS = (((1,), (1,)), ((), ()))
TN_DIMS = (((0,), (0,)), ((), ()))


def _const_spec(shape):
    nd = len(shape)
    return pl.BlockSpec(shape, lambda *_: (0,) * nd, pipeline_mode=pl.Buffered(1))


def _stage_in_body(x_ref, cs_ref, gmix_ref, wa_ref, wg_ref, gq_ref, gkv_ref, wq_ref,
                   *rest, nb, tt, absorb, c_q, c_kv, conv_ch):
    if absorb:
        (wukt_ref,) = rest[:1]
        q_ref, qlat_ref, lat_ref, kr_ref, u_ref, sg_ref = rest[1:]
    else:
        wkn_ref, wv_ref = rest[:2]
        q_ref, k_ref, v_ref, lat_ref, kr_ref, u_ref, sg_ref = rest[2:]

    rows = nb * tt
    d = x_ref.shape[-1]

    x = x_ref[...].reshape(rows, d)
    h = _rms(x, gmix_ref[...]).astype(BF16)
    z = _dot(h, wa_ref[...])
    o_kv = c_q
    o_ks = o_kv + c_kv
    o_u = o_ks + HEAD_SLAB

    sg_ref[...] = jax.nn.sigmoid(_dot(h, wg_ref[...])).astype(BF16).reshape(nb, tt, sg_ref.shape[-1])

    cos_q, sin_q, cos_k, sin_k = (cs_ref[:, i * HEAD_SLAB:(i + 1) * HEAD_SLAB] for i in range(4))
    half = QK_ROPE // 2
    first_half = lax.broadcasted_iota(jnp.int32, (rows, HEAD_SLAB), 1) < QK_NOPE + half

    def rope3(a, cos, sin):
        swapped = jnp.where(first_half, pltpu.roll(a, HEAD_SLAB - half, axis=1), pltpu.roll(a, half, axis=1))
        a3 = a.reshape(nb, tt, HEAD_SLAB) * cos[None] + swapped.reshape(nb, tt, HEAD_SLAB) * sin[None]
        return a3.reshape(rows, HEAD_SLAB)

    cqn = _rms(z[:, :c_q], gq_ref[...]).astype(BF16)
    qa = _dot(cqn, wq_ref[...])
    q_heads = []
    for hd in range(N_HEADS):
        sl = slice(hd * HEAD_SLAB, (hd + 1) * HEAD_SLAB)
        qh = rope3(qa[:, sl], cos_q, sin_q).astype(BF16)
        q_heads.append(qh)
        q_ref[:, :, sl] = qh.reshape(nb, tt, HEAD_SLAB)

    latent = _rms(z[:, o_kv:o_ks], gkv_ref[...])
    lat_ref[...] = latent.reshape(nb, tt, c_kv)
    kslab = rope3(z[:, o_ks:o_u], cos_k, sin_k)
    if absorb:
        kr_ref[...] = kslab[:, QK_NOPE:QK_NOPE + QK_ROPE].reshape(nb, tt, QK_ROPE)
    else:
        kr_ref[0] = kslab.T[QK_NOPE:QK_NOPE + QK_ROPE, :]

    if absorb:
        for hd in range(N_HEADS):
            ql = _dot(q_heads[hd], wukt_ref[hd])
            qlat_ref[:, :, hd * c_kv:(hd + 1) * c_kv] = ql.astype(BF16).reshape(nb, tt, c_kv)
    else:
        lat_b = latent.astype(BF16)
        kn = _dot(lat_b, wkn_ref[...])
        for hd in range(N_HEADS):
            sl = slice(hd * HEAD_SLAB, (hd + 1) * HEAD_SLAB)
            k_ref[:, :, sl] = (kn[:, sl] + kslab).astype(BF16).reshape(nb, tt, HEAD_SLAB)
        v_ref[0] = lax.dot_general(wv_ref[...], lat_b, NT_DIMS, preferred_element_type=F32).astype(BF16)

    u = z[:, o_u:o_u + conv_ch] * jax.nn.sigmoid(z[:, o_u + conv_ch:o_u + 2 * conv_ch])
    u_ref[...] = u.reshape(nb, tt, conv_ch)


def _stage_in(x, cs, wts, *, nb, tt, absorb):
    b, t_len, d = x.shape
    c_q = wts["gq"].shape[-1]
    c_kv = wts["gkv"].shape[-1]
    conv_ch = wts["bconv"].shape[-1]
    n_gate = wts["wg"].shape[-1]
    grid = (b // nb, t_len // tt)

    def row_spec(width):
        return pl.BlockSpec((nb, tt, width), lambda i, j: (i, j, 0))

    in_specs = [row_spec(d), pl.BlockSpec((tt, 4 * HEAD_SLAB), lambda i, j: (j, 0))]
    names = ["gmix", "wa", "wg", "gq", "gkv", "wq"]
    names += ["wukt"] if absorb else ["wkn", "wv"]
    in_specs += [_const_spec(wts[n].shape) for n in names]

    qk_w = N_HEADS * HEAD_SLAB
    out_shapes = [jax.ShapeDtypeStruct((b, t_len, qk_w), BF16)]
    out_specs = [row_spec(qk_w)]
    if absorb:
        out_shapes.append(jax.ShapeDtypeStruct((b, t_len, N_HEADS * c_kv), BF16))
        out_specs.append(row_spec(N_HEADS * c_kv))
    else:
        assert nb == 1, "transposed value tiles are written one sequence at a time"
        out_shapes += [jax.ShapeDtypeStruct((b, t_len, qk_w), BF16),
                       jax.ShapeDtypeStruct((b, N_HEADS * V_DIM, t_len), BF16)]
        out_specs += [row_spec(qk_w), pl.BlockSpec((1, N_HEADS * V_DIM, tt), lambda i, j: (i, 0, j))]
    if absorb:
        kr_shape, kr_spec = jax.ShapeDtypeStruct((b, t_len, QK_ROPE), F32), row_spec(QK_ROPE)
    else:
        kr_shape = jax.ShapeDtypeStruct((b, QK_ROPE, t_len), F32)
        kr_spec = pl.BlockSpec((1, QK_ROPE, tt), lambda i, j: (i, 0, j))
    out_shapes += [jax.ShapeDtypeStruct((b, t_len, c_kv), F32), kr_shape,
                   jax.ShapeDtypeStruct((b, t_len, conv_ch), F32),
                   jax.ShapeDtypeStruct((b, t_len, n_gate), BF16)]
    out_specs += [row_spec(c_kv), kr_spec, row_spec(conv_ch), row_spec(n_gate)]

    body = functools.partial(_stage_in_body, nb=nb, tt=tt, absorb=absorb, c_q=c_q, c_kv=c_kv,
                             conv_ch=conv_ch)
    return pl.pallas_call(
        body,
        grid=grid,
        in_specs=in_specs,
        out_specs=out_specs,
        out_shape=out_shapes,
        compiler_params=pltpu.CompilerParams(dimension_semantics=("arbitrary", "arbitrary"),
                                             vmem_limit_bytes=VMEM_LIMIT),
        name="stage_in_absorb" if absorb else "stage_in",
    )(x, cs, *[wts[n] for n in names])


def _attn_prompt_body(q_ref, k_ref, vt_ref, o_ref, sa_ref, sb_ref, m_ref, acc_ref, *, tq, qc, groups):
    n_col = tq // qc
    slab = lambda hh: slice(hh * HEAD_SLAB, (hh + 1) * HEAD_SLAB)
    ones_rows = jnp.ones((ACC_ROWS - V_DIM, tq), BF16)

    def scores_into(qt, kt, s_buf, diag):
        for hh in range(2):
            if not diag:
                k = k_ref[0, kt * tq:(kt + 1) * tq, slab(hh)]
                s_buf[hh] = lax.dot_general(k, q_ref[0, qt * tq:(qt + 1) * tq, slab(hh)], NT_DIMS,
                                            preferred_element_type=F32)
                continue
            for cb in range(n_col):
                n_keys = (cb + 1) * qc
                k = k_ref[0, kt * tq:kt * tq + n_keys, slab(hh)]
                q = q_ref[0, qt * tq + cb * qc:qt * tq + (cb + 1) * qc, slab(hh)]
                s_buf[hh, :n_keys, cb * qc:(cb + 1) * qc] = lax.dot_general(
                    k, q, NT_DIMS, preferred_element_type=F32)

    def update(c, s, vt, mask):
        if mask is not None:
            s = jnp.where(mask, s, NEG_INF)
        m = m_ref[c]
        m_new = jnp.maximum(m, jnp.max(s, axis=0, keepdims=True))
        p = jnp.exp2(s - m_new).astype(BF16)
        vt_aug = jnp.concatenate([vt, ones_rows[:, :vt.shape[1]]], axis=0)
        acc_ref[c] = jnp.exp2(m - m_new) * acc_ref[c] + _dot(vt_aug, p)
        m_ref[c] = m_new

    def full_step(qt, kt, s_cur, s_nxt, next_is_diag):
        scores_into(qt, kt + 1, s_nxt, next_is_diag)
        for hh in range(2):
            vt = vt_ref[0, hh * V_DIM:(hh + 1) * V_DIM, kt * tq:(kt + 1) * tq]
            for cb in range(n_col):
                update(hh * n_col + cb, s_cur[hh, :, cb * qc:(cb + 1) * qc], vt, None)

    def diag_step(qt, s_cur):
        for hh in range(2):
            for cb in range(n_col):
                c = hh * n_col + cb
                n_keys = (cb + 1) * qc
                vt = vt_ref[0, hh * V_DIM:(hh + 1) * V_DIM, qt * tq:qt * tq + n_keys]
                key_chunk = lax.broadcasted_iota(jnp.int32, (n_keys, qc), 0) // CHUNK
                qry_chunk = (lax.broadcasted_iota(jnp.int32, (n_keys, qc), 1) + cb * qc) // CHUNK
                update(c, s_cur[hh, :n_keys, cb * qc:(cb + 1) * qc], vt, key_chunk <= qry_chunk)
                acc = acc_ref[c]
                o_ref[0, hh * V_DIM:(hh + 1) * V_DIM, qt * tq + cb * qc:qt * tq + (cb + 1) * qc] = (
                    acc[:V_DIM] / acc[V_DIM:V_DIM + 1]).astype(BF16)

    bufs = (sa_ref, sb_ref)
    for g, tiles in enumerate(groups):
        @pl.when(pl.program_id(2) == g)
        def _(tiles=tiles):
            for qt in tiles:
                m_ref[...] = jnp.full(m_ref.shape, NEG_INF, F32)
                acc_ref[...] = jnp.zeros(acc_ref.shape, F32)
                scores_into(qt, 0, bufs[0], qt == 0)
                for kt in range(qt):
                    full_step(qt, kt, bufs[kt % 2], bufs[(kt + 1) % 2], kt + 1 == qt)
                diag_step(qt, bufs[qt % 2])


def _attn_prompt(q, k, vt, *, tq, qc):
    b, t_len, _ = q.shape
    n_q = t_len // tq
    groups = ((0, 1),) + tuple((t,) for t in range(2, n_q)) if n_q > 1 else ((0,),)
    n_chains = 2 * (tq // qc)
    return pl.pallas_call(
        functools.partial(_attn_prompt_body, tq=tq, qc=qc, groups=groups),
        grid=(b, N_HEADS // 2, len(groups)),
        in_specs=[pl.BlockSpec((1, t_len, 2 * HEAD_SLAB), lambda i, p, j: (i, 0, p)),
                  pl.BlockSpec((1, t_len, 2 * HEAD_SLAB), lambda i, p, j: (i, 0, p)),
                  pl.BlockSpec((1, 2 * V_DIM, t_len), lambda i, p, j: (i, p, 0))],
        out_specs=pl.BlockSpec((1, 2 * V_DIM, t_len), lambda i, p, j: (i, p, 0)),
        out_shape=jax.ShapeDtypeStruct((b, N_HEADS * V_DIM, t_len), BF16),
        scratch_shapes=[pltpu.VMEM((2, tq, tq), F32), pltpu.VMEM((2, tq, tq), F32),
                        pltpu.VMEM((n_chains, 1, qc), F32), pltpu.VMEM((n_chains, ACC_ROWS, qc), F32)],
        compiler_params=pltpu.CompilerParams(dimension_semantics=("arbitrary",) * 3,
                                             vmem_limit_bytes=VMEM_LIMIT),
        name="attn_prompt",
    )(q, k, vt)


def _attn_sample_body(q_ref, qlat_ref, clat_ref, ckrt_ref, nlat_ref, nkr_ref, o_ref, *, past, tt, c_kv):
    for i in range(q_ref.shape[0]):
        _attn_sample_one(i, q_ref, qlat_ref, clat_ref, ckrt_ref, nlat_ref, nkr_ref, o_ref, past, tt, c_kv)


def _attn_sample_one(i, q_ref, qlat_ref, clat_ref, ckrt_ref, nlat_ref, nkr_ref, o_ref, past, tt, c_kv):
    ql = jnp.concatenate([qlat_ref[i, :, hd * c_kv:(hd + 1) * c_kv] for hd in range(N_HEADS)], axis=0)
    qs = jnp.concatenate([q_ref[i, :, hd * HEAD_SLAB:(hd + 1) * HEAD_SLAB] for hd in range(N_HEADS)],
                         axis=0)
    qr = qs[:, QK_NOPE:QK_NOPE + QK_ROPE]

    clat = clat_ref[i].astype(BF16)
    ckrt = ckrt_ref[i].astype(BF16)
    nlat = nlat_ref[i].astype(BF16)
    nkr = nkr_ref[i].astype(BF16)
    s_c = (lax.dot_general(ql, clat, NT_DIMS, preferred_element_type=F32)
           + _dot(qr, ckrt)) * SM_SCALE
    s_n = (lax.dot_general(ql, nlat, NT_DIMS, preferred_element_type=F32)
           + lax.dot_general(qr, nkr, NT_DIMS, preferred_element_type=F32)) * SM_SCALE
    rows = N_HEADS * tt
    q_pos = past + lax.broadcasted_iota(jnp.int32, (rows, tt), 0) % tt
    k_pos = past + lax.broadcasted_iota(jnp.int32, (rows, tt), 1)
    s_n = jnp.where(k_pos // CHUNK <= q_pos // CHUNK, s_n, NEG_INF)

    m = jnp.maximum(jnp.max(s_c, axis=-1, keepdims=True), jnp.max(s_n, axis=-1, keepdims=True))
    p_c = jnp.exp(s_c - m)
    p_n = jnp.exp(s_n - m)
    l = jnp.sum(p_c, axis=-1, keepdims=True) + jnp.sum(p_n, axis=-1, keepdims=True)
    o = (_dot(p_c.astype(BF16), clat) + _dot(p_n.astype(BF16), nlat)) / l
    for hd in range(N_HEADS):
        o_ref[i, :, hd * c_kv:(hd + 1) * c_kv] = o[hd * tt:(hd + 1) * tt, :].astype(BF16)


def _attn_sample(q, qlat, clat, ckrt, nlat, nkr):
    b, tt, _ = q.shape
    past, c_kv = clat.shape[1], clat.shape[2]
    bb = SAMPLE_BB if b % SAMPLE_BB == 0 else 1

    def spec(a):
        return pl.BlockSpec((bb,) + a.shape[1:], lambda i: (i, 0, 0))

    return pl.pallas_call(
        functools.partial(_attn_sample_body, past=past, tt=tt, c_kv=c_kv),
        grid=(b // bb,),
        in_specs=[spec(a) for a in (q, qlat, clat, ckrt, nlat, nkr)],
        out_specs=pl.BlockSpec((bb, tt, N_HEADS * c_kv), lambda i: (i, 0, 0)),
        out_shape=jax.ShapeDtypeStruct((b, tt, N_HEADS * c_kv), BF16),
        compiler_params=pltpu.CompilerParams(dimension_semantics=("arbitrary",),
                                             vmem_limit_bytes=VMEM_LIMIT),
        name="attn_sample",
    )(q, qlat, clat, ckrt, nlat, nkr)


def _stage_out_body(x_ref, o_ref, u_ref, hist_ref, sg_ref, p_ref, bconv_ref, wconv_ref, lng_ref, lnb_ref,
                    wao_ref, wco_ref, wout_ref, gffn_ref, wup_ref, wdn_ref, gple_ref, wpg_ref, wpp_ref, gfin_ref,
                    *rest, nb, tt, tiles_per_seq, absorb, ff_chunk):
    if absorb:
        wuv_ref, y_ref, uext_ref, shift_ref = rest
    else:
        y_ref, uext_ref, shift_ref = rest
    t = pl.program_id(0) % tiles_per_seq
    x = x_ref[...]
    d = x.shape[-1]
    conv_ch = u_ref.shape[-1]

    @pl.when(t == 0)
    def _():
        uext_ref[:, :HIST_ROWS, :] = hist_ref[...]

    uext_ref[:, HIST_ROWS:, :] = u_ref[...]
    dw = jnp.zeros((nb, tt, conv_ch), F32) + bconv_ref[...][None]
    for r in range(SUBLANES):
        offs = [o for o in range(HIST_PAD, HIST_PAD + CONV_W) if o % SUBLANES == r]
        src_ref = uext_ref
        if r:
            n_rows = tt + max(offs) - r
            shift_ref[:, :n_rows, :] = uext_ref[:, r:r + n_rows, :]
            src_ref = shift_ref
        for o in offs:
            k = o - HIST_PAD
            dw = dw + src_ref[:, o - r:o - r + tt, :] * wconv_ref[k:k + 1, :][None]
    uext_ref[:, :HIST_ROWS, :] = uext_ref[:, tt:tt + HIST_ROWS, :]
    mu = jnp.mean(dw, axis=-1, keepdims=True)
    dc = dw - mu
    ln = dc * lax.rsqrt(jnp.mean(dc * dc, axis=-1, keepdims=True) + LN_EPS)
    ln = ln * lng_ref[...][None] + lnb_ref[...][None]
    cact = (ln * jax.nn.sigmoid(ln)).astype(BF16).reshape(nb * tt, conv_ch)

    if absorb:
        o = _dot(o_ref[...], wuv_ref[...]).astype(BF16)
        a = _dot(o, wao_ref[...])
    else:
        a = lax.dot_general(o_ref[0], wao_ref[...], TN_DIMS, preferred_element_type=F32)
    cb = _dot(cact, wco_ref[...])
    m = sg_ref[:, :d] * a + sg_ref[:, d:] * cb
    x = x + _dot(m.astype(BF16), wout_ref[...])

    h = _rms(x, gffn_ref[...]).astype(BF16)
    d_ff = wup_ref.shape[-1]
    ff = jnp.zeros_like(x)
    for c in range(d_ff // ff_chunk):
        sl = slice(c * ff_chunk, (c + 1) * ff_chunk)
        up = jnp.maximum(_dot(h, wup_ref[:, sl]), 0.0)
        ff = ff + _dot((up * up).astype(BF16), wdn_ref[sl, :])
    x = x + ff

    h = _rms(x, gple_ref[...]).astype(BF16)
    pg = jax.nn.sigmoid(_dot(h, wpg_ref[...]))
    x = x + pg * _dot(p_ref[...].astype(BF16), wpp_ref[...])
    y_ref[...] = _rms(x, gfin_ref[...])


def _stage_out(x, o, u, hist, sg, p, wts, *, nb, tt, absorb):
    n, d = x.shape
    b, t_len, conv_ch = u.shape
    tm = nb * tt
    tiles_per_seq = t_len // tt
    names = ["bconv", "wconv", "lng", "lnb", "wao", "wco", "wout", "gffn", "wup", "wdn", "gple", "wpg", "wpp",
             "gfin"]
    if absorb:
        names.append("wuv_bd")

    def row_spec(a):
        return pl.BlockSpec((tm, a.shape[-1]), lambda i: (i, 0))

    if absorb:
        o_spec = row_spec(o)
    else:
        o_spec = pl.BlockSpec((1, o.shape[1], tm), lambda i: (i // tiles_per_seq, 0, i % tiles_per_seq))
    u_spec = pl.BlockSpec((nb, tt, conv_ch), lambda i: (i // tiles_per_seq, i % tiles_per_seq, 0))
    hist_spec = pl.BlockSpec((nb, HIST_ROWS, conv_ch), lambda i: (i // tiles_per_seq, 0, 0))

    return pl.pallas_call(
        functools.partial(_stage_out_body, nb=nb, tt=tt, tiles_per_seq=tiles_per_seq, absorb=absorb,
                          ff_chunk=1024),
        grid=(n // tm,),
        in_specs=[row_spec(x), o_spec, u_spec, hist_spec, row_spec(sg), row_spec(p)]
        + [_const_spec(wts[k].shape) for k in names],
        out_specs=pl.BlockSpec((tm, d), lambda i: (i, 0)),
        out_shape=jax.ShapeDtypeStruct((n, d), F32),
        scratch_shapes=[pltpu.VMEM((nb, tt + HIST_ROWS, conv_ch), F32),
                        pltpu.VMEM((nb, tt + HIST_ROWS, conv_ch), F32)],
        compiler_params=pltpu.CompilerParams(dimension_semantics=("arbitrary",),
                                             vmem_limit_bytes=VMEM_LIMIT),
        name="stage_out_absorb" if absorb else "stage_out",
    )(x, o, u, hist, sg, p, *[wts[k] for k in names])


def _head_slabs(nope, rope):
    c, hds = (nope if nope is not None else rope).shape[:2]
    parts = [nope if nope is not None else jnp.zeros((c, hds, QK_NOPE), F32),
             rope if rope is not None else jnp.zeros((c, hds, QK_ROPE), F32),
             jnp.zeros((c, hds, HEAD_SLAB - QK_NOPE - QK_ROPE), F32)]
    return jnp.concatenate(parts, axis=-1).reshape(c, hds * HEAD_SLAB)


def _prep_weights(norm_mix_g, w_in, q_norm_g, w_uq, kv_norm_g, w_uk, w_uv, w_attn_out, conv_w, conv_b,
                  conv_ln_g, conv_ln_b, w_conv_out, w_out, norm_ffn_g, w_ff_up, w_ff_down, ple_norm_g,
                  w_ple_gate, w_ple_proj, final_norm_g):
    d = w_in.shape[0]
    c_q = q_norm_g.shape[-1]
    c_kv = kv_norm_g.shape[-1]
    conv_ch = conv_b.shape[-1]
    o_kv = c_q
    o_kr = o_kv + c_kv
    o_conv = o_kr + QK_ROPE
    o_gate = o_conv + 2 * conv_ch
    w_kr = w_in[:, o_kr:o_conv]
    zn = jnp.zeros((d, QK_NOPE), F32)
    zp = jnp.zeros((d, HEAD_SLAB - QK_NOPE - QK_ROPE), F32)
    wa = jnp.concatenate([w_in[:, :o_kr], zn, w_kr, zp, w_in[:, o_conv:o_gate]], axis=1)
    q_nope, q_rope = w_uq[..., :QK_NOPE], w_uq[..., QK_NOPE:]
    row = lambda v: v.reshape(1, -1).astype(F32)
    wukt = jnp.transpose(w_uk, (1, 2, 0))
    wukt = jnp.concatenate([wukt, jnp.zeros((N_HEADS, HEAD_SLAB - QK_NOPE, c_kv), F32)], axis=1)
    eye = jnp.eye(N_HEADS, dtype=F32)
    wuv_bd = jnp.einsum("chd,hg->hcgd", w_uv, eye).reshape(N_HEADS * c_kv, N_HEADS * V_DIM)
    return {
        "gmix": row(norm_mix_g), "wa": wa.astype(BF16), "gq": row(q_norm_g), "gkv": row(kv_norm_g),
        "wq": _head_slabs(q_nope, q_rope).astype(BF16),
        "wkn": _head_slabs(w_uk, None).astype(BF16),
        "wv": w_uv.reshape(c_kv, N_HEADS * V_DIM).T.astype(BF16),
        "wukt": wukt.astype(BF16), "wuv_bd": wuv_bd.astype(BF16),
        "bconv": row(conv_b), "wconv": conv_w.astype(F32), "lng": row(conv_ln_g), "lnb": row(conv_ln_b),
        "wg": w_in[:, o_gate:].astype(BF16), "wao": w_attn_out.astype(BF16),
        "wco": w_conv_out.astype(BF16), "wout": w_out.astype(BF16), "gffn": row(norm_ffn_g),
        "wup": w_ff_up.astype(BF16), "wdn": w_ff_down.astype(BF16), "gple": row(ple_norm_g),
        "wpg": w_ple_gate.astype(BF16), "wpp": w_ple_proj.astype(BF16), "gfin": row(final_norm_g),
    }


def _rope_tables(pos, q_scale):
    half = QK_ROPE // 2
    inv = ROPE_THETA ** (-jnp.arange(half, dtype=F32) / half)
    ang = pos.astype(F32)[:, None] * inv[None, :]
    cos, sin = jnp.cos(ang), jnp.sin(ang)
    lanes = ((0, 0), (QK_NOPE, HEAD_SLAB - QK_NOPE - QK_ROPE))
    cos_t = jnp.pad(jnp.concatenate([cos, cos], axis=1), lanes)
    cos_t = cos_t + (jnp.arange(HEAD_SLAB) < QK_NOPE).astype(F32)[None, :]
    sin_t = jnp.pad(jnp.concatenate([-sin, sin], axis=1), lanes)
    return jnp.concatenate([cos_t * q_scale, sin_t * q_scale, cos_t, sin_t], axis=1)


def _conv_cache(hist, u):
    n_hist = CONV_W - 1
    t_len = u.shape[1]
    if t_len >= n_hist:
        return u[:, t_len - n_hist:]
    return jnp.concatenate([hist[:, HIST_ROWS - (n_hist - t_len):], u], axis=1)


def _tile(n, pref):
    return pref if n % pref == 0 else n


def kernel(x_prompt, x_sample, p_prompt, p_sample, cache_kv_latent, cache_k_rope, cache_conv, norm_mix_g, w_in, q_norm_g, w_uq, kv_norm_g, w_uk, w_uv, w_attn_out, conv_w, conv_b, conv_ln_g, conv_ln_b, w_conv_out, w_out, norm_ffn_g, w_ff_up, w_ff_down, ple_norm_g, w_ple_gate, w_ple_proj, final_norm_g):
    depth = w_in.shape[0]
    assert depth == 1, "one layer: the two request groups are independent within it"
    bp, tp, d = x_prompt.shape
    bs, ts, _ = x_sample.shape
    past = cache_kv_latent.shape[2]
    conv_ch = conv_b.shape[-1]
    wts = _prep_weights(norm_mix_g[0], w_in[0], q_norm_g[0], w_uq[0], kv_norm_g[0], w_uk[0], w_uv[0],
                        w_attn_out[0], conv_w[0], conv_b[0], conv_ln_g[0], conv_ln_b[0], w_conv_out[0],
                        w_out[0], norm_ffn_g[0], w_ff_up[0], w_ff_down[0], ple_norm_g[0], w_ple_gate[0],
                        w_ple_proj[0], final_norm_g)

    tt = _tile(tp, 512)
    hist0 = jnp.zeros((bp, HIST_ROWS, conv_ch), F32)
    q, k, vt, lat_p, krt_p, u_p, sg_p = _stage_in(x_prompt, _rope_tables(jnp.arange(tp), EXP2_SCALE), wts,
                                                  nb=1, tt=tt, absorb=False)
    o_p = _attn_prompt(q, k, vt, tq=ATTN_TQ, qc=ATTN_QC)
    n_p = bp * tp
    y_p = _stage_out(x_prompt.reshape(n_p, d), o_p, u_p, hist0, sg_p.reshape(n_p, -1),
                     p_prompt[0].reshape(n_p, -1), wts, nb=1, tt=tt, absorb=False)

    hist_s = jnp.pad(cache_conv[0], ((0, 0), (HIST_PAD, 0), (0, 0)))
    qs, qlat, lat_s, kr_s, u_s, sg_s = _stage_in(x_sample, _rope_tables(past + jnp.arange(ts), 1.0), wts,
                                                 nb=bs, tt=ts, absorb=True)
    o_s = _attn_sample(qs, qlat, cache_kv_latent[0], jnp.swapaxes(cache_k_rope[0], 1, 2), lat_s, kr_s)
    n_s = bs * ts
    y_s = _stage_out(x_sample.reshape(n_s, d), o_s.reshape(n_s, -1), u_s, hist_s, sg_s.reshape(n_s, -1),
                     p_sample[0].reshape(n_s, -1), wts, nb=bs, tt=ts, absorb=True)

    ncv_p = _conv_cache(hist0, u_p)
    ncv_s = _conv_cache(hist_s, u_s)

    kr_p = jnp.swapaxes(krt_p, 1, 2)
    return (y_p.reshape(bp, tp, d), y_s.reshape(bs, ts, d), lat_p[None], kr_p[None], ncv_p[None],
            lat_s[None], kr_s[None], ncv_s[None])
```

```python
import functools
import math

import jax
import jax.numpy as jnp
from jax import lax
from jax.experimental import pallas as pl
from jax.experimental.pallas import tpu as pltpu

F32 = jnp.float32
BF16 = jnp.bfloat16

CHUNK = 64
N_HEADS = 8
QK_NOPE = 64
QK_ROPE = 32
V_DIM = 64
CONV_W = 31
ROPE_THETA = 10000.0
RMS_EPS = 1e-6
LN_EPS = 1e-5
SM_SCALE = (QK_NOPE + QK_ROPE) ** -0.5
EXP2_SCALE = SM_SCALE * math.log2(math.e)
NEG_INF = -1e30

LANES = 128
SUBLANES = 8
HEAD_SLAB = LANES
HIST_ROWS = 32
HIST_PAD = HIST_ROWS - (CONV_W - 1)
ATTN_TQ = 1024
ATTN_QC = 256
SAMPLE_BB = 2
BF16_SUBLANES = 16
ACC_ROWS = V_DIM + BF16_SUBLANES
VMEM_LIMIT = 56 * 1024 * 1024


def _rms(x, g):
    return x * lax.rsqrt(jnp.mean(x * x, axis=-1, keepdims=True) + RMS_EPS) * g


def _dot(a, b):
    return jnp.dot(a, b, preferred_element_type=F32)


NT_DIMS = (((1,), (1,)), ((), ()))
TN_DIMS = (((0,), (0,)), ((), ()))


def _const_spec(shape):
    nd = len(shape)
    return pl.BlockSpec(shape, lambda *_: (0,) * nd, pipeline_mode=pl.Buffered(1))


def _stage_in_body(x_ref, cs_ref, gmix_ref, wa_ref, wg_ref, gq_ref, gkv_ref, wq_ref,
                   *rest, nb, tt, absorb, c_q, c_kv, conv_ch):
    if absorb:
        (wukt_ref,) = rest[:1]
        q_ref, qlat_ref, lat_ref, kr_ref, u_ref, sg_ref = rest[1:]
    else:
        wkn_ref, wv_ref = rest[:2]
        q_ref, k_ref, v_ref, lat_ref, kr_ref, u_ref, sg_ref = rest[2:]

    rows = nb * tt
    d = x_ref.shape[-1]

    x = x_ref[...].reshape(rows, d)
    h = _rms(x, gmix_ref[...]).astype(BF16)
    z = _dot(h, wa_ref[...])
    o_kv = c_q
    o_ks = o_kv + c_kv
    o_u = o_ks + HEAD_SLAB

    sg_ref[...] = jax.nn.sigmoid(_dot(h, wg_ref[...])).astype(BF16).reshape(nb, tt, sg_ref.shape[-1])

    cos_q, sin_q, cos_k, sin_k = (cs_ref[:, i * HEAD_SLAB:(i + 1) * HEAD_SLAB] for i in range(4))
    half = QK_ROPE // 2
    first_half = lax.broadcasted_iota(jnp.int32, (rows, HEAD_SLAB), 1) < QK_NOPE + half

    def rope3(a, cos, sin):
        swapped = jnp.where(first_half, pltpu.roll(a, HEAD_SLAB - half, axis=1), pltpu.roll(a, half, axis=1))
        a3 = a.reshape(nb, tt, HEAD_SLAB) * cos[None] + swapped.reshape(nb, tt, HEAD_SLAB) * sin[None]
        return a3.reshape(rows, HEAD_SLAB)

    cqn = _rms(z[:, :c_q], gq_ref[...]).astype(BF16)
    qa = _dot(cqn, wq_ref[...])
    q_heads = []
    for hd in range(N_HEADS):
        sl = slice(hd * HEAD_SLAB, (hd + 1) * HEAD_SLAB)
        qh = rope3(qa[:, sl], cos_q, sin_q).astype(BF16)
        q_heads.append(qh)
        q_ref[:, :, sl] = qh.reshape(nb, tt, HEAD_SLAB)

    latent = _rms(z[:, o_kv:o_ks], gkv_ref[...])
    lat_ref[...] = latent.reshape(nb, tt, c_kv)
    kslab = rope3(z[:, o_ks:o_u], cos_k, sin_k)
    if absorb:
        kr_ref[...] = kslab[:, QK_NOPE:QK_NOPE + QK_ROPE].reshape(nb, tt, QK_ROPE)
    else:
        kr_ref[0] = kslab.T[QK_NOPE:QK_NOPE + QK_ROPE, :]

    if absorb:
        for hd in range(N_HEADS):
            ql = _dot(q_heads[hd], wukt_ref[hd])
            qlat_ref[:, :, hd * c_kv:(hd + 1) * c_kv] = ql.astype(BF16).reshape(nb, tt, c_kv)
    else:
        lat_b = latent.astype(BF16)
        kn = _dot(lat_b, wkn_ref[...])
        for hd in range(N_HEADS):
            sl = slice(hd * HEAD_SLAB, (hd + 1) * HEAD_SLAB)
            k_ref[:, :, sl] = (kn[:, sl] + kslab).astype(BF16).reshape(nb, tt, HEAD_SLAB)
        v_ref[0] = lax.dot_general(wv_ref[...], lat_b, NT_DIMS, preferred_element_type=F32).astype(BF16)

    u = z[:, o_u:o_u + conv_ch] * jax.nn.sigmoid(z[:, o_u + conv_ch:o_u + 2 * conv_ch])
    u_ref[...] = u.reshape(nb, tt, conv_ch)


def _stage_in(x, cs, wts, *, nb, tt, absorb):
    b, t_len, d = x.shape
    c_q = wts["gq"].shape[-1]
    c_kv = wts["gkv"].shape[-1]
    conv_ch = wts["bconv"].shape[-1]
    n_gate = wts["wg"].shape[-1]
    grid = (b // nb, t_len // tt)

    def row_spec(width):
        return pl.BlockSpec((nb, tt, width), lambda i, j: (i, j, 0))

    in_specs = [row_spec(d), pl.BlockSpec((tt, 4 * HEAD_SLAB), lambda i, j: (j, 0))]
    names = ["gmix", "wa", "wg", "gq", "gkv", "wq"]
    names += ["wukt"] if absorb else ["wkn", "wv"]
    in_specs += [_const_spec(wts[n].shape) for n in names]

    qk_w = N_HEADS * HEAD_SLAB
    out_shapes = [jax.ShapeDtypeStruct((b, t_len, qk_w), BF16)]
    out_specs = [row_spec(qk_w)]
    if absorb:
        out_shapes.append(jax.ShapeDtypeStruct((b, t_len, N_HEADS * c_kv), BF16))
        out_specs.append(row_spec(N_HEADS * c_kv))
    else:
        assert nb == 1, "transposed value tiles are written one sequence at a time"
        out_shapes += [jax.ShapeDtypeStruct((b, t_len, qk_w), BF16),
                       jax.ShapeDtypeStruct((b, N_HEADS * V_DIM, t_len), BF16)]
        out_specs += [row_spec(qk_w), pl.BlockSpec((1, N_HEADS * V_DIM, tt), lambda i, j: (i, 0, j))]
    if absorb:
        kr_shape, kr_spec = jax.ShapeDtypeStruct((b, t_len, QK_ROPE), F32), row_spec(QK_ROPE)
    else:
        kr_shape = jax.ShapeDtypeStruct((b, QK_ROPE, t_len), F32)
        kr_spec = pl.BlockSpec((1, QK_ROPE, tt), lambda i, j: (i, 0, j))
    out_shapes += [jax.ShapeDtypeStruct((b, t_len, c_kv), F32), kr_shape,
                   jax.ShapeDtypeStruct((b, t_len, conv_ch), F32),
                   jax.ShapeDtypeStruct((b, t_len, n_gate), BF16)]
    out_specs += [row_spec(c_kv), kr_spec, row_spec(conv_ch), row_spec(n_gate)]

    body = functools.partial(_stage_in_body, nb=nb, tt=tt, absorb=absorb, c_q=c_q, c_kv=c_kv,
                             conv_ch=conv_ch)
    return pl.pallas_call(
        body,
        grid=grid,
        in_specs=in_specs,
        out_specs=out_specs,
        out_shape=out_shapes,
        compiler_params=pltpu.CompilerParams(dimension_semantics=("arbitrary", "arbitrary"),
                                             vmem_limit_bytes=VMEM_LIMIT),
        name="stage_in_absorb" if absorb else "stage_in",
    )(x, cs, *[wts[n] for n in names])


def _attn_prompt_body(q_ref, k_ref, vt_ref, o_ref, sa_ref, sb_ref, m_ref, acc_ref, *, tq, qc, groups):
    n_col = tq // qc
    slab = lambda hh: slice(hh * HEAD_SLAB, (hh + 1) * HEAD_SLAB)
    ones_rows = jnp.ones((ACC_ROWS - V_DIM, tq), BF16)

    def scores_into(qt, kt, s_buf, diag):
        for hh in range(2):
            if not diag:
                k = k_ref[0, kt * tq:(kt + 1) * tq, slab(hh)]
                s_buf[hh] = lax.dot_general(k, q_ref[0, qt * tq:(qt + 1) * tq, slab(hh)], NT_DIMS,
                                            preferred_element_type=F32)
                continue
            for cb in range(n_col):
                n_keys = (cb + 1) * qc
                k = k_ref[0, kt * tq:kt * tq + n_keys, slab(hh)]
                q = q_ref[0, qt * tq + cb * qc:qt * tq + (cb + 1) * qc, slab(hh)]
                s_buf[hh, :n_keys, cb * qc:(cb + 1) * qc] = lax.dot_general(
                    k, q, NT_DIMS, preferred_element_type=F32)

    def update(c, s, vt, mask):
        if mask is not None:
            s = jnp.where(mask, s, NEG_INF)
        m = m_ref[c]
        m_new = jnp.maximum(m, jnp.max(s, axis=0, keepdims=True))
        p = jnp.exp2(s - m_new).astype(BF16)
        vt_aug = jnp.concatenate([vt, ones_rows[:, :vt.shape[1]]], axis=0)
        acc_ref[c] = jnp.exp2(m - m_new) * acc_ref[c] + _dot(vt_aug, p)
        m_ref[c] = m_new

    def full_step(qt, kt, s_cur, s_nxt, next_is_diag):
        scores_into(qt, kt + 1, s_nxt, next_is_diag)
        for hh in range(2):
            vt = vt_ref[0, hh * V_DIM:(hh + 1) * V_DIM, kt * tq:(kt + 1) * tq]
            for cb in range(n_col):
                update(hh * n_col + cb, s_cur[hh, :, cb * qc:(cb + 1) * qc], vt, None)

    def diag_step(qt, s_cur):
        for hh in range(2):
            for cb in range(n_col):
                c = hh * n_col + cb
                n_keys = (cb + 1) * qc
                vt = vt_ref[0, hh * V_DIM:(hh + 1) * V_DIM, qt * tq:qt * tq + n_keys]
                key_chunk = lax.broadcasted_iota(jnp.int32, (n_keys, qc), 0) // CHUNK
                qry_chunk = (lax.broadcasted_iota(jnp.int32, (n_keys, qc), 1) + cb * qc) // CHUNK
                update(c, s_cur[hh, :n_keys, cb * qc:(cb + 1) * qc], vt, key_chunk <= qry_chunk)
                acc = acc_ref[c]
                o_ref[0, hh * V_DIM:(hh + 1) * V_DIM, qt * tq + cb * qc:qt * tq + (cb + 1) * qc] = (
                    acc[:V_DIM] / acc[V_DIM:V_DIM + 1]).astype(BF16)

    bufs = (sa_ref, sb_ref)
    for g, tiles in enumerate(groups):
        @pl.when(pl.program_id(2) == g)
        def _(tiles=tiles):
            for qt in tiles:
                m_ref[...] = jnp.full(m_ref.shape, NEG_INF, F32)
                acc_ref[...] = jnp.zeros(acc_ref.shape, F32)
                scores_into(qt, 0, bufs[0], qt == 0)
                for kt in range(qt):
                    full_step(qt, kt, bufs[kt % 2], bufs[(kt + 1) % 2], kt + 1 == qt)
                diag_step(qt, bufs[qt % 2])


def _attn_prompt(q, k, vt, *, tq, qc):
    b, t_len, _ = q.shape
    n_q = t_len // tq
    groups = ((0, 1, 2),) + tuple((t,) for t in range(3, n_q)) if n_q > 2 else (tuple(range(n_q)),)
    n_chains = 2 * (tq // qc)
    return pl.pallas_call(
        functools.partial(_attn_prompt_body, tq=tq, qc=qc, groups=groups),
        grid=(b, N_HEADS // 2, len(groups)),
        in_specs=[pl.BlockSpec((1, t_len, 2 * HEAD_SLAB), lambda i, p, j: (i, 0, p)),
                  pl.BlockSpec((1, t_len, 2 * HEAD_SLAB), lambda i, p, j: (i, 0, p)),
                  pl.BlockSpec((1, 2 * V_DIM, t_len), lambda i, p, j: (i, p, 0))],
        out_specs=pl.BlockSpec((1, 2 * V_DIM, t_len), lambda i, p, j: (i, p, 0)),
        out_shape=jax.ShapeDtypeStruct((b, N_HEADS * V_DIM, t_len), BF16),
        scratch_shapes=[pltpu.VMEM((2, tq, tq), F32), pltpu.VMEM((2, tq, tq), F32),
                        pltpu.VMEM((n_chains, 1, qc), F32), pltpu.VMEM((n_chains, ACC_ROWS, qc), F32)],
        compiler_params=pltpu.CompilerParams(dimension_semantics=("arbitrary",) * 3,
                                             vmem_limit_bytes=VMEM_LIMIT),
        name="attn_prompt",
    )(q, k, vt)


def _attn_sample_body(q_ref, qlat_ref, clat_ref, ckrt_ref, nlat_ref, nkr_ref, o_ref, *, past, tt, c_kv):
    for i in range(q_ref.shape[0]):
        _attn_sample_one(i, q_ref, qlat_ref, clat_ref, ckrt_ref, nlat_ref, nkr_ref, o_ref, past, tt, c_kv)


def _attn_sample_one(i, q_ref, qlat_ref, clat_ref, ckrt_ref, nlat_ref, nkr_ref, o_ref, past, tt, c_kv):
    ql = jnp.concatenate([qlat_ref[i, :, hd * c_kv:(hd + 1) * c_kv] for hd in range(N_HEADS)], axis=0)
    qs = jnp.concatenate([q_ref[i, :, hd * HEAD_SLAB:(hd + 1) * HEAD_SLAB] for hd in range(N_HEADS)],
                         axis=0)
    qr = qs[:, QK_NOPE:QK_NOPE + QK_ROPE]

    clat = clat_ref[i].astype(BF16)
    ckrt = ckrt_ref[i].astype(BF16)
    nlat = nlat_ref[i].astype(BF16)
    nkr = nkr_ref[i].astype(BF16)
    s_c = (lax.dot_general(ql, clat, NT_DIMS, preferred_element_type=F32)
           + _dot(qr, ckrt)) * SM_SCALE
    s_n = (lax.dot_general(ql, nlat, NT_DIMS, preferred_element_type=F32)
           + lax.dot_general(qr, nkr, NT_DIMS, preferred_element_type=F32)) * SM_SCALE
    rows = N_HEADS * tt
    q_pos = past + lax.broadcasted_iota(jnp.int32, (rows, tt), 0) % tt
    k_pos = past + lax.broadcasted_iota(jnp.int32, (rows, tt), 1)
    s_n = jnp.where(k_pos // CHUNK <= q_pos // CHUNK, s_n, NEG_INF)

    m = jnp.maximum(jnp.max(s_c, axis=-1, keepdims=True), jnp.max(s_n, axis=-1, keepdims=True))
    p_c = jnp.exp(s_c - m)
    p_n = jnp.exp(s_n - m)
    l = jnp.sum(p_c, axis=-1, keepdims=True) + jnp.sum(p_n, axis=-1, keepdims=True)
    o = (_dot(p_c.astype(BF16), clat) + _dot(p_n.astype(BF16), nlat)) / l
    for hd in range(N_HEADS):
        o_ref[i, :, hd * c_kv:(hd + 1) * c_kv] = o[hd * tt:(hd + 1) * tt, :].astype(BF16)


def _attn_sample(q, qlat, clat, ckrt, nlat, nkr):
    b, tt, _ = q.shape
    past, c_kv = clat.shape[1], clat.shape[2]
    bb = SAMPLE_BB if b % SAMPLE_BB == 0 else 1

    def spec(a):
        return pl.BlockSpec((bb,) + a.shape[1:], lambda i: (i, 0, 0))

    return pl.pallas_call(
        functools.partial(_attn_sample_body, past=past, tt=tt, c_kv=c_kv),
        grid=(b // bb,),
        in_specs=[spec(a) for a in (q, qlat, clat, ckrt, nlat, nkr)],
        out_specs=pl.BlockSpec((bb, tt, N_HEADS * c_kv), lambda i: (i, 0, 0)),
        out_shape=jax.ShapeDtypeStruct((b, tt, N_HEADS * c_kv), BF16),
        compiler_params=pltpu.CompilerParams(dimension_semantics=("arbitrary",),
                                             vmem_limit_bytes=VMEM_LIMIT),
        name="attn_sample",
    )(q, qlat, clat, ckrt, nlat, nkr)


def _stage_out_body(x_ref, o_ref, u_ref, hist_ref, sg_ref, p_ref, bconv_ref, wconv_ref, lng_ref, lnb_ref,
                    wao_ref, wco_ref, wout_ref, gffn_ref, wup_ref, wdn_ref, gple_ref, wpg_ref, wpp_ref, gfin_ref,
                    *rest, nb, tt, tiles_per_seq, absorb, ff_chunk):
    if absorb:
        wuv_ref, y_ref, uext_ref, shift_ref = rest
    else:
        y_ref, uext_ref, shift_ref = rest
    t = pl.program_id(0) % tiles_per_seq
    x = x_ref[...]
    d = x.shape[-1]
    conv_ch = u_ref.shape[-1]

    @pl.when(t == 0)
    def _():
        uext_ref[:, :HIST_ROWS, :] = hist_ref[...]

    uext_ref[:, HIST_ROWS:, :] = u_ref[...]
    dw = jnp.zeros((nb, tt, conv_ch), F32) + bconv_ref[...][None]
    for r in range(SUBLANES):
        offs = [o for o in range(HIST_PAD, HIST_PAD + CONV_W) if o % SUBLANES == r]
        src_ref = uext_ref
        if r:
            n_rows = tt + max(offs) - r
            shift_ref[:, :n_rows, :] = uext_ref[:, r:r + n_rows, :]
            src_ref = shift_ref
        for o in offs:
            k = o - HIST_PAD
            dw = dw + src_ref[:, o - r:o - r + tt, :] * wconv_ref[k:k + 1, :][None]
    uext_ref[:, :HIST_ROWS, :] = uext_ref[:, tt:tt + HIST_ROWS, :]
    mu = jnp.mean(dw, axis=-1, keepdims=True)
    dc = dw - mu
    ln = dc * lax.rsqrt(jnp.mean(dc * dc, axis=-1, keepdims=True) + LN_EPS)
    ln = ln * lng_ref[...][None] + lnb_ref[...][None]
    cact = (ln * jax.nn.sigmoid(ln)).astype(BF16).reshape(nb * tt, conv_ch)

    if absorb:
        o = _dot(o_ref[...], wuv_ref[...]).astype(BF16)
        a = _dot(o, wao_ref[...])
    else:
        a = lax.dot_general(o_ref[0], wao_ref[...], TN_DIMS, preferred_element_type=F32)
    cb = _dot(cact, wco_ref[...])
    m = sg_ref[:, :d] * a + sg_ref[:, d:] * cb
    x = x + _dot(m.astype(BF16), wout_ref[...])

    h = _rms(x, gffn_ref[...]).astype(BF16)
    d_ff = wup_ref.shape[-1]
    ff = jnp.zeros_like(x)
    for c in range(d_ff // ff_chunk):
        sl = slice(c * ff_chunk, (c + 1) * ff_chunk)
        up = jnp.maximum(_dot(h, wup_ref[:, sl]), 0.0)
        ff = ff + _dot((up * up).astype(BF16), wdn_ref[sl, :])
    x = x + ff

    h = _rms(x, gple_ref[...]).astype(BF16)
    pg = jax.nn.sigmoid(_dot(h, wpg_ref[...]))
    x = x + pg * _dot(p_ref[...].astype(BF16), wpp_ref[...])
    y_ref[...] = _rms(x, gfin_ref[...])


def _stage_out(x, o, u, hist, sg, p, wts, *, nb, tt, absorb):
    n, d = x.shape
    b, t_len, conv_ch = u.shape
    tm = nb * tt
    tiles_per_seq = t_len // tt
    names = ["bconv", "wconv", "lng", "lnb", "wao", "wco", "wout", "gffn", "wup", "wdn", "gple", "wpg", "wpp",
             "gfin"]
    if absorb:
        names.append("wuv_bd")

    def row_spec(a):
        return pl.BlockSpec((tm, a.shape[-1]), lambda i: (i, 0))

    if absorb:
        o_spec = row_spec(o)
    else:
        o_spec = pl.BlockSpec((1, o.shape[1], tm), lambda i: (i // tiles_per_seq, 0, i % tiles_per_seq))
    u_spec = pl.BlockSpec((nb, tt, conv_ch), lambda i: (i // tiles_per_seq, i % tiles_per_seq, 0))
    hist_spec = pl.BlockSpec((nb, HIST_ROWS, conv_ch), lambda i: (i // tiles_per_seq, 0, 0))

    return pl.pallas_call(
        functools.partial(_stage_out_body, nb=nb, tt=tt, tiles_per_seq=tiles_per_seq, absorb=absorb,
                          ff_chunk=1024),
        grid=(n // tm,),
        in_specs=[row_spec(x), o_spec, u_spec, hist_spec, row_spec(sg), row_spec(p)]
        + [_const_spec(wts[k].shape) for k in names],
        out_specs=pl.BlockSpec((tm, d), lambda i: (i, 0)),
        out_shape=jax.ShapeDtypeStruct((n, d), F32),
        scratch_shapes=[pltpu.VMEM((nb, tt + HIST_ROWS, conv_ch), F32),
                        pltpu.VMEM((nb, tt + HIST_ROWS, conv_ch), F32)],
        compiler_params=pltpu.CompilerParams(dimension_semantics=("arbitrary",),
                                             vmem_limit_bytes=VMEM_LIMIT),
        name="stage_out_absorb" if absorb else "stage_out",
    )(x, o, u, hist, sg, p, *[wts[k] for k in names])


def _head_slabs(nope, rope):
    c, hds = (nope if nope is not None else rope).shape[:2]
    parts = [nope if nope is not None else jnp.zeros((c, hds, QK_NOPE), F32),
             rope if rope is not None else jnp.zeros((c, hds, QK_ROPE), F32),
             jnp.zeros((c, hds, HEAD_SLAB - QK_NOPE - QK_ROPE), F32)]
    return jnp.concatenate(parts, axis=-1).reshape(c, hds * HEAD_SLAB)


def _prep_weights(norm_mix_g, w_in, q_norm_g, w_uq, kv_norm_g, w_uk, w_uv, w_attn_out, conv_w, conv_b,
                  conv_ln_g, conv_ln_b, w_conv_out, w_out, norm_ffn_g, w_ff_up, w_ff_down, ple_norm_g,
                  w_ple_gate, w_ple_proj, final_norm_g):
    d = w_in.shape[0]
    c_q = q_norm_g.shape[-1]
    c_kv = kv_norm_g.shape[-1]
    conv_ch = conv_b.shape[-1]
    o_kv = c_q
    o_kr = o_kv + c_kv
    o_conv = o_kr + QK_ROPE
    o_gate = o_conv + 2 * conv_ch
    w_kr = w_in[:, o_kr:o_conv]
    zn = jnp.zeros((d, QK_NOPE), F32)
    zp = jnp.zeros((d, HEAD_SLAB - QK_NOPE - QK_ROPE), F32)
    wa = jnp.concatenate([w_in[:, :o_kr], zn, w_kr, zp, w_in[:, o_conv:o_gate]], axis=1)
    q_nope, q_rope = w_uq[..., :QK_NOPE], w_uq[..., QK_NOPE:]
    row = lambda v: v.reshape(1, -1).astype(F32)
    wukt = jnp.transpose(w_uk, (1, 2, 0))
    wukt = jnp.concatenate([wukt, jnp.zeros((N_HEADS, HEAD_SLAB - QK_NOPE, c_kv), F32)], axis=1)
    eye = jnp.eye(N_HEADS, dtype=F32)
    wuv_bd = jnp.einsum("chd,hg->hcgd", w_uv, eye).reshape(N_HEADS * c_kv, N_HEADS * V_DIM)
    return {
        "gmix": row(norm_mix_g), "wa": wa.astype(BF16), "gq": row(q_norm_g), "gkv": row(kv_norm_g),
        "wq": _head_slabs(q_nope, q_rope).astype(BF16),
        "wkn": _head_slabs(w_uk, None).astype(BF16),
        "wv": w_uv.reshape(c_kv, N_HEADS * V_DIM).T.astype(BF16),
        "wukt": wukt.astype(BF16), "wuv_bd": wuv_bd.astype(BF16),
        "bconv": row(conv_b), "wconv": conv_w.astype(F32), "lng": row(conv_ln_g), "lnb": row(conv_ln_b),
        "wg": w_in[:, o_gate:].astype(BF16), "wao": w_attn_out.astype(BF16),
        "wco": w_conv_out.astype(BF16), "wout": w_out.astype(BF16), "gffn": row(norm_ffn_g),
        "wup": w_ff_up.astype(BF16), "wdn": w_ff_down.astype(BF16), "gple": row(ple_norm_g),
        "wpg": w_ple_gate.astype(BF16), "wpp": w_ple_proj.astype(BF16), "gfin": row(final_norm_g),
    }


def _rope_tables(pos, q_scale):
    half = QK_ROPE // 2
    inv = ROPE_THETA ** (-jnp.arange(half, dtype=F32) / half)
    ang = pos.astype(F32)[:, None] * inv[None, :]
    cos, sin = jnp.cos(ang), jnp.sin(ang)
    lanes = ((0, 0), (QK_NOPE, HEAD_SLAB - QK_NOPE - QK_ROPE))
    cos_t = jnp.pad(jnp.concatenate([cos, cos], axis=1), lanes)
    cos_t = cos_t + (jnp.arange(HEAD_SLAB) < QK_NOPE).astype(F32)[None, :]
    sin_t = jnp.pad(jnp.concatenate([-sin, sin], axis=1), lanes)
    return jnp.concatenate([cos_t * q_scale, sin_t * q_scale, cos_t, sin_t], axis=1)


def _conv_cache(hist, u):
    n_hist = CONV_W - 1
    t_len = u.shape[1]
    if t_len >= n_hist:
        return u[:, t_len - n_hist:]
    return jnp.concatenate([hist[:, HIST_ROWS - (n_hist - t_len):], u], axis=1)


def _tile(n, pref):
    return pref if n % pref == 0 else n


def kernel(x_prompt, x_sample, p_prompt, p_sample, cache_kv_latent, cache_k_rope, cache_conv, norm_mix_g, w_in, q_norm_g, w_uq, kv_norm_g, w_uk, w_uv, w_attn_out, conv_w, conv_b, conv_ln_g, conv_ln_b, w_conv_out, w_out, norm_ffn_g, w_ff_up, w_ff_down, ple_norm_g, w_ple_gate, w_ple_proj, final_norm_g):
    depth = w_in.shape[0]
    assert depth == 1, "one layer: the two request groups are independent within it"
    bp, tp, d = x_prompt.shape
    bs, ts, _ = x_sample.shape
    past = cache_kv_latent.shape[2]
    conv_ch = conv_b.shape[-1]
    wts = _prep_weights(norm_mix_g[0], w_in[0], q_norm_g[0], w_uq[0], kv_norm_g[0], w_uk[0], w_uv[0],
                        w_attn_out[0], conv_w[0], conv_b[0], conv_ln_g[0], conv_ln_b[0], w_conv_out[0],
                        w_out[0], norm_ffn_g[0], w_ff_up[0], w_ff_down[0], ple_norm_g[0], w_ple_gate[0],
                        w_ple_proj[0], final_norm_g)

    tt = _tile(tp, 512)
    hist0 = jnp.zeros((bp, HIST_ROWS, conv_ch), F32)
    q, k, vt, lat_p, krt_p, u_p, sg_p = _stage_in(x_prompt, _rope_tables(jnp.arange(tp), EXP2_SCALE), wts,
                                                  nb=1, tt=tt, absorb=False)
    o_p = _attn_prompt(q, k, vt, tq=ATTN_TQ, qc=ATTN_QC)
    n_p = bp * tp
    y_p = _stage_out(x_prompt.reshape(n_p, d), o_p, u_p, hist0, sg_p.reshape(n_p, -1),
                     p_prompt[0].reshape(n_p, -1), wts, nb=1, tt=tt, absorb=False)

    hist_s = jnp.pad(cache_conv[0], ((0, 0), (HIST_PAD, 0), (0, 0)))
    qs, qlat, lat_s, kr_s, u_s, sg_s = _stage_in(x_sample, _rope_tables(past + jnp.arange(ts), 1.0), wts,
                                                 nb=bs, tt=ts, absorb=True)
    o_s = _attn_sample(qs, qlat, cache_kv_latent[0], jnp.swapaxes(cache_k_rope[0], 1, 2), lat_s, kr_s)
    n_s = bs * ts
    y_s = _stage_out(x_sample.reshape(n_s, d), o_s.reshape(n_s, -1), u_s, hist_s, sg_s.reshape(n_s, -1),
                     p_sample[0].reshape(n_s, -1), wts, nb=bs, tt=ts, absorb=True)

    ncv_p = _conv_cache(hist0, u_p)
    ncv_s = _conv_cache(hist_s, u_s)

    kr_p = jnp.swapaxes(krt_p, 1, 2)
    return (y_p.reshape(bp, tp, d), y_s.reshape(bs, ts, d), lat_p[None], kr_p[None], ncv_p[None],
            lat_s[None], kr_s[None], ncv_s[None])
```
